```python
import math
import jax, jax.numpy as jnp
from jax import lax
import numpy as np

D_MODEL = 2048
BATCH = 4
SEQ = 2048
DEPTH = 1
DEC_BATCH = 128
DEC_SEQ = 4
PAST_LEN = 16384
PAGE_SIZE = 128

E_CONV = D_MODEL // 2
CONV_W = 3
E_SSM = D_MODEL // 2
GROUP = 16
N_GROUPS = E_SSM // GROUP
P_STATE = 64
N_HEADS = 4
HEAD_DIM = D_MODEL // 8
E_ATTN = N_HEADS * HEAD_DIM
MEM_LEN = 256
N_BRANCH = 3
N_IN = 4 * E_CONV + 2 * E_SSM + 2 * E_ATTN + N_BRANCH * D_MODEL
EPS = 1e-6

kernel_name = "hybrid_conv_s5_memxattn_decode_step"


def _rmsnorm(x, g):
    x32 = x.astype(jnp.float32)
    y = x32 * lax.rsqrt(jnp.mean(x32 * x32, axis=-1, keepdims=True) + EPS)
    return (y * g.astype(jnp.float32)).astype(x.dtype)


def _split_points():
    sizes = [E_CONV, E_CONV, E_CONV, E_CONV, E_SSM, E_SSM, E_ATTN, E_ATTN]
    return list(np.cumsum(sizes))


def _combine(e1, e2):
    a1r, a1i, b1r, b1i = e1
    a2r, a2i, b2r, b2i = e2
    ar = a1r * a2r - a1i * a2i
    ai = a1r * a2i + a1i * a2r
    br = a2r * b1r - a2i * b1i + b2r
    bi = a2r * b1i + a2i * b1r + b2i
    return (ar, ai, br, bi)


def _s5(u, s0_re, s0_im, lam_re, lam_im, log_dt, b_re, b_im, c_re, c_im, d_skip):
    bsz, L, _ = u.shape
    u32 = u.astype(jnp.float32).reshape(bsz, L, N_GROUPS, GROUP)
    dt = jnp.exp(log_dt.astype(jnp.float32))[:, None]
    lr = lam_re.astype(jnp.float32)
    li = lam_im.astype(jnp.float32)
    mag = jnp.exp(lr * dt)
    ang = li * dt
    ar = mag * jnp.cos(ang)
    ai = mag * jnp.sin(ang)
    den = lr * lr + li * li
    fr = ((ar - 1.0) * lr + ai * li) / den
    fi = (ai * lr - (ar - 1.0) * li) / den
    br32 = b_re.astype(jnp.float32)
    bi32 = b_im.astype(jnp.float32)
    bbr = fr[..., None] * br32 - fi[..., None] * bi32
    bbi = fr[..., None] * bi32 + fi[..., None] * br32
    xr = jnp.einsum('blgc,gpc->blgp', u32, bbr)
    xi = jnp.einsum('blgc,gpc->blgp', u32, bbi)
    s0r = s0_re.astype(jnp.float32)
    s0i = s0_im.astype(jnp.float32)
    xr = xr.at[:, 0].add(ar * s0r - ai * s0i)
    xi = xi.at[:, 0].add(ar * s0i + ai * s0r)
    a_r = jnp.broadcast_to(ar, xr.shape)
    a_i = jnp.broadcast_to(ai, xr.shape)
    _, _, sr, si = lax.associative_scan(_combine, (a_r, a_i, xr, xi), axis=1)
    y = (jnp.einsum('blgp,gcp->blgc', sr, c_re.astype(jnp.float32))
         - jnp.einsum('blgp,gcp->blgc', si, c_im.astype(jnp.float32)))
    y = y.reshape(bsz, L, E_SSM) + d_skip.astype(jnp.float32) * u32.reshape(bsz, L, E_SSM)
    return y.astype(u.dtype), sr[:, -1], si[:, -1]


def _layer(x, mem_k, mem_v, conv_prev, s_re, s_im, p):
    bsz, L, _ = x.shape
    h = _rmsnorm(x, p['norm_g'])
    proj = h @ p['w_in']
    cb, cc, ch, cz, su, sz, aq, az, gpre = jnp.split(proj, _split_points(), axis=-1)

    v = cc * ch
    pad = jnp.concatenate([conv_prev.astype(v.dtype), v], axis=1)
    w = p['conv_w']
    conv = w[0] * pad[:, :L] + w[1] * pad[:, 1:L + 1] + w[2] * pad[:, 2:L + 2]
    conv_out = (cb * conv * jax.nn.silu(cz)) @ p['w_conv_out']
    new_conv = pad[:, L:]

    ys, new_re, new_im = _s5(su, s_re, s_im, p['lam_re'], p['lam_im'], p['log_dt'],
                             p['b_re'], p['b_im'], p['c_re'], p['c_im'], p['d_skip'])
    ys = jax.nn.gelu(ys * jax.nn.silu(sz))
    ssm_out = (ys @ p['w_glu_a']) * jax.nn.sigmoid(ys @ p['w_glu_b'])

    q = aq.reshape(bsz, L, N_HEADS, HEAD_DIM)
    scores = jnp.einsum('blhd,bmhd->bhlm', q, mem_k).astype(jnp.float32) * (HEAD_DIM ** -0.5)
    probs = jax.nn.softmax(scores, axis=-1).astype(mem_v.dtype)
    o = jnp.einsum('bhlm,bmhd->blhd', probs, mem_v).reshape(bsz, L, E_ATTN)
    attn_out = (o * jax.nn.silu(az)) @ p['w_attn_out']

    g = jax.nn.sigmoid(gpre).reshape(bsz, L, N_BRANCH, D_MODEL)
    merged = g[:, :, 0] * conv_out + g[:, :, 1] * ssm_out + g[:, :, 2] * attn_out
    return x + merged @ p['w_out'], new_conv, new_re, new_im


def setup_inputs(seed: int = 0) -> dict:
    key = jax.random.key(seed)
    ks = jax.random.split(key, 32)
    f = jnp.float32
    nrm = lambda k, shape, s: jax.random.normal(k, shape, f) * s
    lam_im0 = math.pi * jnp.arange(P_STATE, dtype=f)
    return {
        'x_prompt': nrm(ks[0], (BATCH, SEQ, D_MODEL), 1.0),
        'x_sample': nrm(ks[1], (DEC_BATCH, DEC_SEQ, D_MODEL), 1.0),
        'mem_prompt': nrm(ks[2], (BATCH, MEM_LEN, D_MODEL), 1.0),
        'cache_mem_k': nrm(ks[3], (DEPTH, DEC_BATCH, MEM_LEN, N_HEADS, HEAD_DIM), 1.0),
        'cache_mem_v': nrm(ks[4], (DEPTH, DEC_BATCH, MEM_LEN, N_HEADS, HEAD_DIM), 1.0),
        'state_conv': nrm(ks[5], (DEPTH, DEC_BATCH, CONV_W - 1, E_CONV), 0.5),
        'state_ssm_re': nrm(ks[6], (DEPTH, DEC_BATCH, N_GROUPS, P_STATE), 0.5),
        'state_ssm_im': nrm(ks[7], (DEPTH, DEC_BATCH, N_GROUPS, P_STATE), 0.5),
        'norm_g': 1.0 + nrm(ks[8], (DEPTH, D_MODEL), 0.02),
        'mem_norm_g': 1.0 + nrm(ks[9], (DEPTH, D_MODEL), 0.02),
        'w_in': nrm(ks[10], (DEPTH, D_MODEL, N_IN), D_MODEL ** -0.5),
        'conv_w': nrm(ks[11], (DEPTH, CONV_W, E_CONV), CONV_W ** -0.5),
        'w_conv_out': nrm(ks[12], (DEPTH, E_CONV, D_MODEL), E_CONV ** -0.5),
        'ssm_lambda_re': -0.5 + nrm(ks[13], (DEPTH, N_GROUPS, P_STATE), 0.01),
        'ssm_lambda_im': lam_im0 + nrm(ks[14], (DEPTH, N_GROUPS, P_STATE), 0.01),
        'ssm_log_dt': jax.random.uniform(ks[15], (DEPTH, N_GROUPS), f, math.log(1e-3), math.log(1e-1)),
        'ssm_b_re': nrm(ks[16], (DEPTH, N_GROUPS, P_STATE, GROUP), GROUP ** -0.5),
        'ssm_b_im': nrm(ks[17], (DEPTH, N_GROUPS, P_STATE, GROUP), GROUP ** -0.5),
        'ssm_c_re': nrm(ks[18], (DEPTH, N_GROUPS, GROUP, P_STATE), P_STATE ** -0.5),
        'ssm_c_im': nrm(ks[19], (DEPTH, N_GROUPS, GROUP, P_STATE), P_STATE ** -0.5),
        'ssm_d': 1.0 + nrm(ks[20], (DEPTH, E_SSM), 0.1),
        'w_glu_a': nrm(ks[21], (DEPTH, E_SSM, D_MODEL), E_SSM ** -0.5),
        'w_glu_b': nrm(ks[22], (DEPTH, E_SSM, D_MODEL), E_SSM ** -0.5),
        'w_mem_k': nrm(ks[23], (DEPTH, D_MODEL, E_ATTN), D_MODEL ** -0.5),
        'w_mem_v': nrm(ks[24], (DEPTH, D_MODEL, E_ATTN), D_MODEL ** -0.5),
        'w_attn_out': nrm(ks[25], (DEPTH, E_ATTN, D_MODEL), E_ATTN ** -0.5),
        'w_out': nrm(ks[26], (DEPTH, D_MODEL, D_MODEL), D_MODEL ** -0.5),
        'final_norm_g': 1.0 + nrm(ks[27], (D_MODEL,), 0.02),
    }


def reference(x_prompt, x_sample, mem_prompt, cache_mem_k, cache_mem_v, state_conv,
              state_ssm_re, state_ssm_im, norm_g, mem_norm_g, w_in, conv_w, w_conv_out,
              ssm_lambda_re, ssm_lambda_im, ssm_log_dt, ssm_b_re, ssm_b_im, ssm_c_re,
              ssm_c_im, ssm_d, w_glu_a, w_glu_b, w_mem_k, w_mem_v, w_attn_out, w_out,
              final_norm_g):
    xp = x_prompt
    xs = x_sample
    mk_p, mv_p, cv_p, sr_p, si_p = [], [], [], [], []
    cv_s, sr_s, si_s = [], [], []
    for l in range(DEPTH):
        p = {
            'norm_g': norm_g[l], 'w_in': w_in[l], 'conv_w': conv_w[l],
            'w_conv_out': w_conv_out[l], 'lam_re': ssm_lambda_re[l],
            'lam_im': ssm_lambda_im[l], 'log_dt': ssm_log_dt[l], 'b_re': ssm_b_re[l],
            'b_im': ssm_b_im[l], 'c_re': ssm_c_re[l], 'c_im': ssm_c_im[l],
            'd_skip': ssm_d[l], 'w_glu_a': w_glu_a[l], 'w_glu_b': w_glu_b[l],
            'w_attn_out': w_attn_out[l], 'w_out': w_out[l],
        }
        mem_n = _rmsnorm(mem_prompt, mem_norm_g[l])
        mk = (mem_n @ w_mem_k[l]).reshape(BATCH, MEM_LEN, N_HEADS, HEAD_DIM)
        mv = (mem_n @ w_mem_v[l]).reshape(BATCH, MEM_LEN, N_HEADS, HEAD_DIM)
        zc = jnp.zeros((BATCH, CONV_W - 1, E_CONV), xp.dtype)
        zs = jnp.zeros((BATCH, N_GROUPS, P_STATE), jnp.float32)
        xp, ncp, nrp, nip = _layer(xp, mk, mv, zc, zs, zs, p)
        mk_p.append(mk); mv_p.append(mv); cv_p.append(ncp); sr_p.append(nrp); si_p.append(nip)
        xs, ncs, nrs, nis = _layer(xs, cache_mem_k[l], cache_mem_v[l], state_conv[l],
                                   state_ssm_re[l], state_ssm_im[l], p)
        cv_s.append(ncs); sr_s.append(nrs); si_s.append(nis)
    y_prompt = _rmsnorm(xp, final_norm_g)
    y_sample = _rmsnorm(xs, final_norm_g)
    return (y_prompt, y_sample, jnp.stack(mk_p), jnp.stack(mv_p), jnp.stack(cv_p),
            jnp.stack(sr_p), jnp.stack(si_p), jnp.stack(cv_s), jnp.stack(sr_s),
            jnp.stack(si_s))
```

```python
import functools
import math

import jax
import jax.numpy as jnp
from jax import lax
from jax.experimental import pallas as pl
from jax.experimental.pallas import tpu as pltpu

D_MODEL = 2048
BATCH = 4
SEQ = 2048
DEC_BATCH = 128
DEC_SEQ = 4
E_CONV = 1024
CONV_W = 3
E_SSM = 1024
GROUP = 16
N_GROUPS = 64
P_STATE = 64
N_HEADS = 4
HEAD_DIM = 256
E_ATTN = 1024
MEM_LEN = 256
N_IN = 14336
EPS = 1e-6

F32 = jnp.float32
BF16 = jnp.bfloat16

CB = 1024
N_ACT_BLOCKS = 11
GROUPS_PER_BLK = 16
N_BLK = N_GROUPS // GROUPS_PER_BLK
BLK_IN = GROUPS_PER_BLK * GROUP
BLK_ST = GROUPS_PER_BLK * P_STATE
SCAN_W = 512
SEGS = 8
VMEM_LIMIT = 56 * 1024 * 1024


def _cparams(sem):
    return pltpu.CompilerParams(dimension_semantics=sem, vmem_limit_bytes=VMEM_LIMIT)


def _rms(x, g):
    ms = jnp.mean(x * x, axis=-1, keepdims=True)
    return x * lax.rsqrt(ms + EPS) * g


def _inproj_kernel(x_ref, g_ref, w_ref, halo_ref, cw_ref, act_ref, nconv_ref,
                   h_scr, cb_scr, vext_scr, *, tm, shift, pad, tiles_per_seq):
    i = pl.program_id(0)
    j = pl.program_id(1)

    @pl.when(j == 0)
    def _():
        h_scr[...] = _rms(x_ref[...], g_ref[...]).astype(BF16)

    def proj():
        return jnp.dot(h_scr[...], w_ref[...], preferred_element_type=F32)

    @pl.when(j == 0)
    def _():
        cb_scr[...] = proj()

    @pl.when(j == 1)
    def _():
        vext_scr[pl.ds(pad, tm), :] = proj()

    @pl.when(j == 2)
    def _():
        vext_scr[pl.ds(pad, tm), :] = vext_scr[pl.ds(pad, tm), :] * proj()

        @pl.when(i % tiles_per_seq == 0)
        def _():
            vext_scr[pl.ds(pad - 2 * shift, 2 * shift), :] = halo_ref[0]

    @pl.when(j == 3)
    def _():
        conv = (cw_ref[0:1, :] * vext_scr[pl.ds(pad - 2 * shift, tm), :]
                + cw_ref[1:2, :] * vext_scr[pl.ds(pad - shift, tm), :]
                + cw_ref[2:3, :] * vext_scr[pl.ds(pad, tm), :])
        act_ref[...] = (cb_scr[...] * conv * jax.nn.silu(proj())).astype(BF16)
        tail = vext_scr[pl.ds(pad + tm - 2 * shift, 2 * shift), :]
        nconv_ref[0] = tail
        vext_scr[pl.ds(pad - 2 * shift, 2 * shift), :] = tail

    @pl.when(j == 4)
    def _():
        act_ref[...] = proj().astype(BF16)

    @pl.when(j == 6)
    def _():
        act_ref[...] = (proj() * (HEAD_DIM ** -0.5)).astype(BF16)

    @pl.when((j == 5) | (j == 7))
    def _():
        act_ref[...] = jax.nn.silu(proj()).astype(BF16)

    @pl.when(j >= 8)
    def _():
        act_ref[...] = jax.nn.sigmoid(proj()).astype(BF16)


def _inproj(x2d, g, w_bf, halo, conv_w, *, tm, shift, pad, tiles_per_seq):
    rows = x2d.shape[0]
    n_seq = rows // (tm * tiles_per_seq)
    kern = functools.partial(_inproj_kernel, tm=tm, shift=shift, pad=pad,
                             tiles_per_seq=tiles_per_seq)
    return pl.pallas_call(
        kern,
        grid=(rows // tm, N_IN // CB),
        in_specs=[
            pl.BlockSpec((tm, D_MODEL), lambda i, j: (i, 0)),
            pl.BlockSpec((1, D_MODEL), lambda i, j: (0, 0)),
            pl.BlockSpec((D_MODEL, CB), lambda i, j: (0, j)),
            pl.BlockSpec((1, 2 * shift, E_CONV), lambda i, j: (i // tiles_per_seq, 0, 0)),
            pl.BlockSpec((CONV_W, E_CONV), lambda i, j: (0, 0)),
        ],
        out_specs=[
            pl.BlockSpec((tm, CB), lambda i, j: (i, jnp.maximum(j - 3, 0))),
            pl.BlockSpec((1, 2 * shift, E_CONV), lambda i, j: (i // tiles_per_seq, 0, 0)),
        ],
        out_shape=[
            jax.ShapeDtypeStruct((rows, N_ACT_BLOCKS * CB), BF16),
            jax.ShapeDtypeStruct((n_seq, 2 * shift, E_CONV), F32),
        ],
        scratch_shapes=[
            pltpu.VMEM((tm, D_MODEL), BF16),
            pltpu.VMEM((tm, E_CONV), F32),
            pltpu.VMEM((pad + tm, E_CONV), F32),
        ],
        compiler_params=_cparams(("arbitrary", "arbitrary")),
        name="inproj",
    )(x2d, g, w_bf, halo, conv_w)


def _memproj_kernel(x_ref, g_ref, w_ref, o_ref):
    h = _rms(x_ref[...], g_ref[...]).astype(BF16)
    o_ref[...] = jnp.dot(h, w_ref[...], preferred_element_type=F32)


def _memproj(mem2d, g, w_bf):
    rows = mem2d.shape[0]
    tm = 512
    return pl.pallas_call(
        _memproj_kernel,
        grid=(rows // tm,),
        in_specs=[
            pl.BlockSpec((tm, D_MODEL), lambda i: (i, 0)),
            pl.BlockSpec((1, D_MODEL), lambda i: (0, 0)),
            pl.BlockSpec((D_MODEL, E_ATTN), lambda i: (0, 0)),
        ],
        out_specs=pl.BlockSpec((tm, E_ATTN), lambda i: (i, 0)),
        out_shape=jax.ShapeDtypeStruct((rows, E_ATTN), F32),
        compiler_params=_cparams(("arbitrary",)),
        name="memproj",
    )(mem2d, g, w_bf)


def _disc_kernel(lr_ref, li_ref, ldt_ref, bre_ref, bim_ref,
                 ar_ref, ai_ref, pr_ref, pi_ref, bbr_ref, bbi_ref, *, n_sq):
    lr = lr_ref[...]
    li = li_ref[...]
    dt = jnp.exp(ldt_ref[...])
    mag = jnp.exp(lr * dt)
    ang = li * dt
    ar = mag * jnp.cos(ang)
    ai = mag * jnp.sin(ang)
    den = lr * lr + li * li
    fr = ((ar - 1.0) * lr + ai * li) / den
    fi = (ai * lr - (ar - 1.0) * li) / den
    bre = bre_ref[...]
    bim = bim_ref[...]
    bbr_ref[...] = fr * bre - fi * bim
    bbi_ref[...] = fr * bim + fi * bre
    ar_ref[...] = ar
    ai_ref[...] = ai
    pr, pi = ar, ai
    for _ in range(n_sq):
        pr, pi = pr * pr - pi * pi, 2.0 * pr * pi
    pr_ref[...] = pr
    pi_ref[...] = pi


def _discretise(lam_re, lam_im, log_dt, b_re, b_im, n_sq):
    rows = N_GROUPS * GROUP
    rep = lambda a: jnp.repeat(a, GROUP, axis=0)
    lr = rep(lam_re)
    li = rep(lam_im)
    ldt = rep(jnp.broadcast_to(log_dt[:, None], (N_GROUPS, P_STATE)))
    bre_t = jnp.transpose(b_re, (0, 2, 1)).reshape(rows, P_STATE)
    bim_t = jnp.transpose(b_im, (0, 2, 1)).reshape(rows, P_STATE)
    sds = jax.ShapeDtypeStruct((rows, P_STATE), F32)
    outs = pl.pallas_call(
        functools.partial(_disc_kernel, n_sq=n_sq),
        out_shape=[sds] * 6,
        name="s5_disc",
    )(lr, li, ldt, bre_t, bim_t)
    ar, ai, pr, pi, bbr_t, bbi_t = outs
    pick = lambda a: a.reshape(N_GROUPS, GROUP, P_STATE)[:, 0, :].reshape(N_BLK, 1, BLK_ST)
    abar = jnp.concatenate([pick(ar), pick(ai)], axis=1)
    apow = jnp.concatenate([pick(pr), pick(pi)], axis=1)
    return abar, apow, bbr_t, bbi_t


def _block_diag_weights(bbr_t, bbi_t, c_re, c_im):
    eye = jnp.eye(GROUPS_PER_BLK, dtype=F32)

    def b_side(a):
        a = a.reshape(N_BLK, GROUPS_PER_BLK, GROUP, P_STATE)
        return jnp.einsum('bgcp,gh->bgchp', a, eye).reshape(N_BLK, BLK_IN, BLK_ST)

    def c_side(a):
        a = a.reshape(N_BLK, GROUPS_PER_BLK, GROUP, P_STATE)
        return jnp.einsum('bgcp,gh->bhpgc', a, eye).reshape(N_BLK, BLK_ST, BLK_IN)

    bblk = jnp.concatenate([b_side(bbr_t), b_side(bbi_t)], axis=2).astype(BF16)
    cblk = jnp.concatenate([c_side(c_re), c_side(-c_im)], axis=1).astype(BF16)
    return bblk, cblk


def _cmul_add(ar, ai, sr, si, xr, xi):
    return ar * sr - ai * si + xr, ar * si + ai * sr + xi


def _ssm_prompt_kernel(u_ref, sz_ref, bblk_ref, cblk_ref, abar_ref, apow_ref, d_ref,
                       ys_ref, sre_ref, sim_ref,
                       x_scr, s_scr, y_scr, ynat_scr, carry_scr, *, tq):
    i = pl.program_id(1)
    seg_len = tq // SEGS
    n_half = BLK_ST // SCAN_W

    @pl.when(i == 0)
    def _():
        carry_scr[...] = jnp.zeros_like(carry_scr)

    u = u_ref[...]
    r = lax.broadcasted_iota(jnp.int32, (tq, tq), 0)
    c = lax.broadcasted_iota(jnp.int32, (tq, tq), 1)
    perm = jnp.where(c == (r % SEGS) * seg_len + r // SEGS, 1.0, 0.0).astype(BF16)
    u_perm = jnp.dot(perm, u, preferred_element_type=F32).astype(BF16)
    row_id = lax.broadcasted_iota(jnp.int32, (SEGS, SCAN_W), 0)

    for blk in range(N_BLK):
        x = jnp.dot(u_perm[:, blk * BLK_IN:(blk + 1) * BLK_IN], bblk_ref[blk],
                    preferred_element_type=F32)
        for cch in range(2 * n_half):
            x_scr[cch] = x[:, cch * SCAN_W:(cch + 1) * SCAN_W]
        for half in range(n_half):
            lo = half * SCAN_W
            ar = jnp.broadcast_to(abar_ref[blk, 0:1, lo:lo + SCAN_W], (SEGS, SCAN_W))
            ai = jnp.broadcast_to(abar_ref[blk, 1:2, lo:lo + SCAN_W], (SEGS, SCAN_W))

            def local_end(j, carry, half=half, ar=ar, ai=ai):
                row = pl.multiple_of(j * SEGS, SEGS)
                xr = x_scr[half, pl.ds(row, SEGS), :]
                xi = x_scr[n_half + half, pl.ds(row, SEGS), :]
                return _cmul_add(ar, ai, carry[0], carry[1], xr, xi)

            zero = jnp.zeros((SEGS, SCAN_W), F32)
            er, ei = lax.fori_loop(0, seg_len, local_end, (zero, zero), unroll=4)

            pr = apow_ref[blk, 0:1, lo:lo + SCAN_W]
            pi = apow_ref[blk, 1:2, lo:lo + SCAN_W]
            prev_r = carry_scr[blk, 0:1, lo:lo + SCAN_W]
            prev_i = carry_scr[blk, 1:2, lo:lo + SCAN_W]
            init_r = zero
            init_i = zero
            for k in range(SEGS):
                init_r = jnp.where(row_id == k, prev_r, init_r)
                init_i = jnp.where(row_id == k, prev_i, init_i)
                prev_r, prev_i = _cmul_add(pr, pi, prev_r, prev_i,
                                           er[k:k + 1, :], ei[k:k + 1, :])
            carry_scr[blk, 0:1, lo:lo + SCAN_W] = prev_r
            carry_scr[blk, 1:2, lo:lo + SCAN_W] = prev_i

            def full_scan(j, carry, half=half, ar=ar, ai=ai):
                row = pl.multiple_of(j * SEGS, SEGS)
                xr = x_scr[half, pl.ds(row, SEGS), :]
                xi = x_scr[n_half + half, pl.ds(row, SEGS), :]
                sr, si = _cmul_add(ar, ai, carry[0], carry[1], xr, xi)
                s_scr[half, pl.ds(row, SEGS), :] = sr
                s_scr[n_half + half, pl.ds(row, SEGS), :] = si
                return sr, si

            lax.fori_loop(0, seg_len, full_scan, (init_r, init_i), unroll=4)

        y = jnp.zeros((tq, BLK_IN), F32)
        for cch in range(2 * n_half):
            y = y + jnp.dot(s_scr[cch].astype(BF16),
                            cblk_ref[blk, cch * SCAN_W:(cch + 1) * SCAN_W, :],
                            preferred_element_type=F32)
        for sl in range(BLK_IN // 128):
            y_scr[blk * (BLK_IN // 128) + sl] = y[:, sl * 128:(sl + 1) * 128]

    for slab in range(E_SSM // 128):
        for k in range(SEGS):
            ynat_scr[k * seg_len:(k + 1) * seg_len, slab * 128:(slab + 1) * 128] = (
                y_scr[slab, pl.ds(k, seg_len, stride=SEGS), :])

    yy = (ynat_scr[...] + d_ref[...] * u.astype(F32)) * sz_ref[...].astype(F32)
    ys_ref[...] = jax.nn.gelu(yy).astype(BF16)
    for blk in range(N_BLK):
        sre_ref[0, :, blk * BLK_ST:(blk + 1) * BLK_ST] = carry_scr[blk, 0:1, :]
        sim_ref[0, :, blk * BLK_ST:(blk + 1) * BLK_ST] = carry_scr[blk, 1:2, :]


def _ssm_prompt(act, bblk, cblk, abar, apow, d_skip, *, tq):
    tiles = SEQ // tq
    n_half = BLK_ST // SCAN_W
    st = jax.ShapeDtypeStruct((BATCH, 1, N_GROUPS * P_STATE), F32)
    const3 = lambda b, i: (0, 0, 0)
    return pl.pallas_call(
        functools.partial(_ssm_prompt_kernel, tq=tq),
        grid=(BATCH, tiles),
        in_specs=[
            pl.BlockSpec((tq, E_SSM), lambda b, i: (b * tiles + i, 1)),
            pl.BlockSpec((tq, E_SSM), lambda b, i: (b * tiles + i, 2)),
            pl.BlockSpec((N_BLK, BLK_IN, 2 * BLK_ST), const3),
            pl.BlockSpec((N_BLK, 2 * BLK_ST, BLK_IN), const3),
            pl.BlockSpec((N_BLK, 2, BLK_ST), const3),
            pl.BlockSpec((N_BLK, 2, BLK_ST), const3),
            pl.BlockSpec((1, E_SSM), lambda b, i: (0, 0)),
        ],
        out_specs=[
            pl.BlockSpec((tq, E_SSM), lambda b, i: (b * tiles + i, 0)),
            pl.BlockSpec((1, 1, N_GROUPS * P_STATE), lambda b, i: (b, 0, 0)),
            pl.BlockSpec((1, 1, N_GROUPS * P_STATE), lambda b, i: (b, 0, 0)),
        ],
        out_shape=[jax.ShapeDtypeStruct((BATCH * SEQ, E_SSM), BF16), st, st],
        scratch_shapes=[
            pltpu.VMEM((2 * n_half, tq, SCAN_W), F32),
            pltpu.VMEM((2 * n_half, tq, SCAN_W), F32),
            pltpu.VMEM((E_SSM // 128, tq, 128), F32),
            pltpu.VMEM((tq, E_SSM), F32),
            pltpu.VMEM((N_BLK, 2, BLK_ST), F32),
        ],
        compiler_params=_cparams(("arbitrary", "arbitrary")),
        name="ssm_prompt",
    )(act, act, bblk, cblk, abar, apow, d_skip)


def _ssm_sample_kernel(u_ref, sz_ref, bblk_ref, cblk_ref, abar_ref, d_ref, s0r_ref, s0i_ref,
                       ys_ref, sre_ref, sim_ref, x_scr, s_scr):
    n_half = BLK_ST // SCAN_W
    rows = DEC_SEQ * DEC_BATCH
    u = u_ref[...]
    ys = []
    for blk in range(N_BLK):
        x = jnp.dot(u[:, blk * BLK_IN:(blk + 1) * BLK_IN], bblk_ref[blk],
                    preferred_element_type=F32)
        for cch in range(2 * n_half):
            x_scr[cch] = x[:, cch * SCAN_W:(cch + 1) * SCAN_W]
        for half in range(n_half):
            lo = half * SCAN_W
            col = blk * BLK_ST + lo
            ar = jnp.broadcast_to(abar_ref[blk, 0:1, lo:lo + SCAN_W], (8, SCAN_W))
            ai = jnp.broadcast_to(abar_ref[blk, 1:2, lo:lo + SCAN_W], (8, SCAN_W))

            def body(rc, carry, half=half, col=col, ar=ar, ai=ai):
                r0 = pl.multiple_of(rc * 8, 8)
                sr = s0r_ref[pl.ds(r0, 8), col:col + SCAN_W]
                si = s0i_ref[pl.ds(r0, 8), col:col + SCAN_W]
                for t in range(DEC_SEQ):
                    row = pl.multiple_of(t * DEC_BATCH + r0, 8)
                    xr = x_scr[half, pl.ds(row, 8), :]
                    xi = x_scr[n_half + half, pl.ds(row, 8), :]
                    sr, si = _cmul_add(ar, ai, sr, si, xr, xi)
                    s_scr[half, pl.ds(row, 8), :] = sr
                    s_scr[n_half + half, pl.ds(row, 8), :] = si
                sre_ref[pl.ds(r0, 8), col:col + SCAN_W] = sr
                sim_ref[pl.ds(r0, 8), col:col + SCAN_W] = si
                return carry

            lax.fori_loop(0, DEC_BATCH // 8, body, 0)

        y = jnp.zeros((rows, BLK_IN), F32)
        for cch in range(2 * n_half):
            y = y + jnp.dot(s_scr[cch].astype(BF16),
                            cblk_ref[blk, cch * SCAN_W:(cch + 1) * SCAN_W, :],
                            preferred_element_type=F32)
        ys.append(y)
    y_all = jnp.concatenate(ys, axis=1)
    yy = (y_all + d_ref[...] * u.astype(F32)) * sz_ref[...].astype(F32)
    ys_ref[...] = jax.nn.gelu(yy).astype(BF16)


def _ssm_sample(act, bblk, cblk, abar, d_skip, s0r, s0i):
    rows = DEC_SEQ * DEC_BATCH
    n_half = BLK_ST // SCAN_W
    st = jax.ShapeDtypeStruct((DEC_BATCH, N_GROUPS * P_STATE), F32)
    full = lambda shape: pl.BlockSpec(shape, lambda i: tuple(0 for _ in shape))
    return pl.pallas_call(
        _ssm_sample_kernel,
        grid=(1,),
        in_specs=[
            pl.BlockSpec((rows, E_SSM), lambda i: (0, 1)),
            pl.BlockSpec((rows, E_SSM), lambda i: (0, 2)),
            full((N_BLK, BLK_IN, 2 * BLK_ST)),
            full((N_BLK, 2 * BLK_ST, BLK_IN)),
            full((N_BLK, 2, BLK_ST)),
            full((1, E_SSM)),
            full((DEC_BATCH, N_GROUPS * P_STATE)),
            full((DEC_BATCH, N_GROUPS * P_STATE)),
        ],
        out_specs=[full((rows, E_SSM)), full((DEC_BATCH, N_GROUPS * P_STATE)),
                   full((DEC_BATCH, N_GROUPS * P_STATE))],
        out_shape=[jax.ShapeDtypeStruct((rows, E_SSM), BF16), st, st],
        scratch_shapes=[
            pltpu.VMEM((2 * n_half, rows, SCAN_W), F32),
            pltpu.VMEM((2 * n_half, rows, SCAN_W), F32),
        ],
        compiler_params=_cparams(("arbitrary",)),
        name="ssm_sample",
    )(act, act, bblk, cblk, abar, d_skip, s0r, s0i)


def _softmax_rows(s):
    m = jnp.max(s, axis=-1, keepdims=True)
    e = jnp.exp(s - m)
    return e / jnp.sum(e, axis=-1, keepdims=True)


def _attn_prompt_kernel(q_ref, az_ref, k_ref, v_ref, o_ref):
    for h in range(N_HEADS):
        cols = slice(h * HEAD_DIM, (h + 1) * HEAD_DIM)
        kh = k_ref[:, cols].astype(BF16)
        vh = v_ref[:, cols].astype(BF16)
        s = lax.dot_general(q_ref[:, cols], kh, (((1,), (1,)), ((), ())),
                            preferred_element_type=F32)
        p = _softmax_rows(s).astype(BF16)
        o = jnp.dot(p, vh, preferred_element_type=F32)
        o_ref[:, cols] = (o * az_ref[:, cols].astype(F32)).astype(BF16)


def _attn_prompt(act, mk2d, mv2d, *, tr):
    tiles = SEQ // tr
    return pl.pallas_call(
        _attn_prompt_kernel,
        grid=(BATCH, tiles),
        in_specs=[
            pl.BlockSpec((tr, E_ATTN), lambda b, i: (b * tiles + i, 3)),
            pl.BlockSpec((tr, E_ATTN), lambda b, i: (b * tiles + i, 4)),
            pl.BlockSpec((MEM_LEN, E_ATTN), lambda b, i: (b, 0)),
            pl.BlockSpec((MEM_LEN, E_ATTN), lambda b, i: (b, 0)),
        ],
        out_specs=pl.BlockSpec((tr, E_ATTN), lambda b, i: (b * tiles + i, 0)),
        out_shape=jax.ShapeDtypeStruct((BATCH * SEQ, E_ATTN), BF16),
        compiler_params=_cparams(("arbitrary", "arbitrary")),
        name="attn_prompt",
    )(act, act, mk2d, mv2d)


def _attn_sample_kernel(q_ref, az_ref, k_ref, v_ref, o_ref, *, bt):
    rows = DEC_SEQ * bt
    q = q_ref[...].reshape(rows, HEAD_DIM)
    k = k_ref[...].reshape(bt * MEM_LEN, HEAD_DIM).astype(BF16)
    v = v_ref[...].reshape(bt * MEM_LEN, HEAD_DIM).astype(BF16)
    s = lax.dot_general(q, k, (((1,), (1,)), ((), ())), preferred_element_type=F32)
    row_req = lax.broadcasted_iota(jnp.int32, s.shape, 0) % bt
    col_req = lax.broadcasted_iota(jnp.int32, s.shape, 1) // MEM_LEN
    s = jnp.where(row_req == col_req, s, -1e30)
    p = _softmax_rows(s).astype(BF16)
    o = jnp.dot(p, v, preferred_element_type=F32)
    az = az_ref[...].reshape(rows, HEAD_DIM).astype(F32)
    o_ref[...] = (o * az).astype(BF16).reshape(DEC_SEQ, bt, HEAD_DIM)


def _attn_sample(act3, k3, v3, *, bt):
    q_blk = 3 * (CB // HEAD_DIM)
    az_blk = 4 * (CB // HEAD_DIM)
    return pl.pallas_call(
        functools.partial(_attn_sample_kernel, bt=bt),
        grid=(DEC_BATCH // bt, N_HEADS),
        in_specs=[
            pl.BlockSpec((DEC_SEQ, bt, HEAD_DIM), lambda b, h: (0, b, q_blk + h)),
            pl.BlockSpec((DEC_SEQ, bt, HEAD_DIM), lambda b, h: (0, b, az_blk + h)),
            pl.BlockSpec((bt, MEM_LEN, HEAD_DIM), lambda b, h: (b, 0, h)),
            pl.BlockSpec((bt, MEM_LEN, HEAD_DIM), lambda b, h: (b, 0, h)),
        ],
        out_specs=pl.BlockSpec((DEC_SEQ, bt, HEAD_DIM), lambda b, h: (0, b, h)),
        out_shape=jax.ShapeDtypeStruct((DEC_SEQ, DEC_BATCH, E_ATTN), BF16),
        compiler_params=_cparams(("arbitrary", "arbitrary")),
        name="attn_sample",
    )(act3, act3, k3, v3)


def _merge_kernel(gated_ref, ys_ref, oz_ref, g0_ref, g1_ref, g2_ref,
                  wco_ref, wga_ref, wgb_ref, wao_ref, o_ref):
    dot = functools.partial(jnp.dot, preferred_element_type=F32)
    conv_out = dot(gated_ref[...], wco_ref[...])
    ys = ys_ref[...]
    ssm_out = dot(ys, wga_ref[...]) * jax.nn.sigmoid(dot(ys, wgb_ref[...]))
    attn_out = dot(oz_ref[...], wao_ref[...])
    merged = (g0_ref[...].astype(F32) * conv_out + g1_ref[...].astype(F32) * ssm_out
              + g2_ref[...].astype(F32) * attn_out)
    o_ref[...] = merged.astype(BF16)


def _merge(act, ys, oz, wco, wga, wgb, wao, *, tm):
    rows = act.shape[0]
    tn = 1024
    nn = D_MODEL // tn
    wspec = pl.BlockSpec((E_CONV, tn), lambda i, n: (0, n))
    return pl.pallas_call(
        _merge_kernel,
        grid=(rows // tm, nn),
        in_specs=[
            pl.BlockSpec((tm, CB), lambda i, n: (i, 0)),
            pl.BlockSpec((tm, E_SSM), lambda i, n: (i, 0)),
            pl.BlockSpec((tm, E_ATTN), lambda i, n: (i, 0)),
            pl.BlockSpec((tm, tn), lambda i, n: (i, 5 + n)),
            pl.BlockSpec((tm, tn), lambda i, n: (i, 5 + nn + n)),
            pl.BlockSpec((tm, tn), lambda i, n: (i, 5 + 2 * nn + n)),
            wspec, wspec, wspec, wspec,
        ],
        out_specs=pl.BlockSpec((tm, tn), lambda i, n: (i, n)),
        out_shape=jax.ShapeDtypeStruct((rows, D_MODEL), BF16),
        compiler_params=_cparams(("arbitrary", "arbitrary")),
        name="merge",
    )(act, ys, oz, act, act, act, wco, wga, wgb, wao)


def _out_kernel(x_ref, m_ref, w_ref, g_ref, o_ref):
    y = x_ref[...] + jnp.dot(m_ref[...], w_ref[...], preferred_element_type=F32)
    o_ref[...] = _rms(y, g_ref[...])


def _outproj(x2d, merged, w_bf, g, *, tm):
    rows = x2d.shape[0]
    return pl.pallas_call(
        _out_kernel,
        grid=(rows // tm,),
        in_specs=[
            pl.BlockSpec((tm, D_MODEL), lambda i: (i, 0)),
            pl.BlockSpec((tm, D_MODEL), lambda i: (i, 0)),
            pl.BlockSpec((D_MODEL, D_MODEL), lambda i: (0, 0)),
            pl.BlockSpec((1, D_MODEL), lambda i: (0, 0)),
        ],
        out_specs=pl.BlockSpec((tm, D_MODEL), lambda i: (i, 0)),
        out_shape=jax.ShapeDtypeStruct((rows, D_MODEL), F32),
        compiler_params=_cparams(("arbitrary",)),
        name="outproj",
    )(x2d, merged, w_bf, g)


def kernel(x_prompt, x_sample, mem_prompt, cache_mem_k, cache_mem_v, state_conv, state_ssm_re, state_ssm_im, norm_g, mem_norm_g, w_in, conv_w, w_conv_out, ssm_lambda_re, ssm_lambda_im, ssm_log_dt, ssm_b_re, ssm_b_im, ssm_c_re, ssm_c_im, ssm_d, w_glu_a, w_glu_b, w_mem_k, w_mem_v, w_attn_out, w_out, final_norm_g):
    l = 0
    tq = 512
    bf = lambda w: w.astype(BF16)
    w_in_bf = bf(w_in[l])
    wco, wga, wgb, wao, wo = (bf(w_conv_out[l]), bf(w_glu_a[l]), bf(w_glu_b[l]),
                              bf(w_attn_out[l]), bf(w_out[l]))
    g_in = norm_g[l][None, :]
    g_fin = final_norm_g[None, :]
    d_skip = ssm_d[l][None, :]

    n_sq = int(math.log2(tq // SEGS))
    abar, apow, bbr_t, bbi_t = _discretise(ssm_lambda_re[l], ssm_lambda_im[l], ssm_log_dt[l],
                                           ssm_b_re[l], ssm_b_im[l], n_sq)
    bblk, cblk = _block_diag_weights(bbr_t, bbi_t, ssm_c_re[l], ssm_c_im[l])

    xp = x_prompt.reshape(BATCH * SEQ, D_MODEL)
    tm_p = 512
    act_p, nconv_p = _inproj(xp, g_in, w_in_bf, jnp.zeros((BATCH, 2, E_CONV), F32), conv_w[l],
                             tm=tm_p, shift=1, pad=8, tiles_per_seq=SEQ // tm_p)
    mem2d = mem_prompt.reshape(BATCH * MEM_LEN, D_MODEL)
    g_mem = mem_norm_g[l][None, :]
    mk = _memproj(mem2d, g_mem, bf(w_mem_k[l]))
    mv = _memproj(mem2d, g_mem, bf(w_mem_v[l]))
    ys_p, sre_p, sim_p = _ssm_prompt(act_p, bblk, cblk, abar, apow, d_skip, tq=tq)
    oz_p = _attn_prompt(act_p, mk, mv, tr=512)
    merged_p = _merge(act_p, ys_p, oz_p, wco, wga, wgb, wao, tm=512)
    y_p = _outproj(xp, merged_p, wo, g_fin, tm=512)

    rows_s = DEC_SEQ * DEC_BATCH
    xs = jnp.transpose(x_sample, (1, 0, 2)).reshape(rows_s, D_MODEL)
    halo_s = jnp.transpose(state_conv[l], (1, 0, 2)).reshape(1, 2 * DEC_BATCH, E_CONV)
    act_s, nconv_s = _inproj(xs, g_in, w_in_bf, halo_s, conv_w[l],
                             tm=rows_s, shift=DEC_BATCH, pad=2 * DEC_BATCH, tiles_per_seq=1)
    s0r = state_ssm_re[l].reshape(DEC_BATCH, N_GROUPS * P_STATE)
    s0i = state_ssm_im[l].reshape(DEC_BATCH, N_GROUPS * P_STATE)
    ys_s, sre_s, sim_s = _ssm_sample(act_s, bblk, cblk, abar, d_skip, s0r, s0i)
    k3 = cache_mem_k[l].reshape(DEC_BATCH, MEM_LEN, E_ATTN)
    v3 = cache_mem_v[l].reshape(DEC_BATCH, MEM_LEN, E_ATTN)
    oz_s = _attn_sample(act_s.reshape(DEC_SEQ, DEC_BATCH, N_ACT_BLOCKS * CB), k3, v3, bt=16)
    merged_s = _merge(act_s, ys_s, oz_s.reshape(rows_s, E_ATTN), wco, wga, wgb, wao, tm=512)
    y_s = _outproj(xs, merged_s, wo, g_fin, tm=512)

    y_prompt = y_p.reshape(BATCH, SEQ, D_MODEL)
    y_sample = jnp.transpose(y_s.reshape(DEC_SEQ, DEC_BATCH, D_MODEL), (1, 0, 2))
    st_shape_p = (1, BATCH, N_GROUPS, P_STATE)
    st_shape_s = (1, DEC_BATCH, N_GROUPS, P_STATE)
    new_conv_s = jnp.transpose(nconv_s.reshape(2, DEC_BATCH, E_CONV), (1, 0, 2))
    return (y_prompt, y_sample,
            mk.reshape(1, BATCH, MEM_LEN, N_HEADS, HEAD_DIM),
            mv.reshape(1, BATCH, MEM_LEN, N_HEADS, HEAD_DIM),
            nconv_p[None],
            sre_p.reshape(st_shape_p), sim_p.reshape(st_shape_p),
            new_conv_s[None],
            sre_s.reshape(st_shape_s), sim_s.reshape(st_shape_s))
```

```python
import functools
import math

import jax
import jax.numpy as jnp
from jax import lax
from jax.experimental import pallas as pl
from jax.experimental.pallas import tpu as pltpu

D_MODEL = 2048
BATCH = 4
SEQ = 2048
DEC_BATCH = 128
DEC_SEQ = 4
E_CONV = 1024
CONV_W = 3
E_SSM = 1024
GROUP = 16
N_GROUPS = 64
P_STATE = 64
N_HEADS = 4
HEAD_DIM = 256
E_ATTN = 1024
MEM_LEN = 256
N_IN = 14336
EPS = 1e-6

F32 = jnp.float32
BF16 = jnp.bfloat16

CB = 1024
N_ACT_BLOCKS = 11
GROUPS_PER_BLK = 16
N_BLK = N_GROUPS // GROUPS_PER_BLK
BLK_IN = GROUPS_PER_BLK * GROUP
BLK_ST = GROUPS_PER_BLK * P_STATE
SCAN_W = 512
SEGS = 8
VMEM_LIMIT = 56 * 1024 * 1024


def _cparams(sem):
    return pltpu.CompilerParams(dimension_semantics=sem, vmem_limit_bytes=VMEM_LIMIT)


def _rms(x, g):
    ms = jnp.mean(x * x, axis=-1, keepdims=True)
    return x * lax.rsqrt(ms + EPS) * g


def _inproj_kernel(x_ref, g_ref, w_ref, halo_ref, cw_ref, act_ref, nconv_ref,
                   h_scr, cb_scr, vext_scr, *, tm, shift, pad, tiles_per_seq):
    i = pl.program_id(0)
    j = pl.program_id(1)

    @pl.when(j == 0)
    def _():
        h_scr[...] = _rms(x_ref[...], g_ref[...]).astype(BF16)

    def proj():
        return jnp.dot(h_scr[...], w_ref[...], preferred_element_type=F32)

    @pl.when(j == 0)
    def _():
        cb_scr[...] = proj()

    @pl.when(j == 1)
    def _():
        vext_scr[pl.ds(pad, tm), :] = proj()

    @pl.when(j == 2)
    def _():
        vext_scr[pl.ds(pad, tm), :] = vext_scr[pl.ds(pad, tm), :] * proj()

        @pl.when(i % tiles_per_seq == 0)
        def _():
            vext_scr[pl.ds(pad - 2 * shift, 2 * shift), :] = halo_ref[0]

    @pl.when(j == 3)
    def _():
        conv = (cw_ref[0:1, :] * vext_scr[pl.ds(pad - 2 * shift, tm), :]
                + cw_ref[1:2, :] * vext_scr[pl.ds(pad - shift, tm), :]
                + cw_ref[2:3, :] * vext_scr[pl.ds(pad, tm), :])
        act_ref[...] = (cb_scr[...] * conv * jax.nn.silu(proj())).astype(BF16)
        tail = vext_scr[pl.ds(pad + tm - 2 * shift, 2 * shift), :]
        nconv_ref[0] = tail
        vext_scr[pl.ds(pad - 2 * shift, 2 * shift), :] = tail

    @pl.when(j == 4)
    def _():
        act_ref[...] = proj().astype(BF16)

    @pl.when(j == 6)
    def _():
        act_ref[...] = (proj() * (HEAD_DIM ** -0.5)).astype(BF16)

    @pl.when((j == 5) | (j == 7))
    def _():
        act_ref[...] = jax.nn.silu(proj()).astype(BF16)

    @pl.when(j >= 8)
    def _():
        act_ref[...] = jax.nn.sigmoid(proj()).astype(BF16)


def _inproj(x2d, g, w_bf, halo, conv_w, *, tm, shift, pad, tiles_per_seq):
    rows = x2d.shape[0]
    n_seq = rows // (tm * tiles_per_seq)
    kern = functools.partial(_inproj_kernel, tm=tm, shift=shift, pad=pad,
                             tiles_per_seq=tiles_per_seq)
    return pl.pallas_call(
        kern,
        grid=(rows // tm, N_IN // CB),
        in_specs=[
            pl.BlockSpec((tm, D_MODEL), lambda i, j: (i, 0)),
            pl.BlockSpec((1, D_MODEL), lambda i, j: (0, 0)),
            pl.BlockSpec((D_MODEL, CB), lambda i, j: (0, j)),
            pl.BlockSpec((1, 2 * shift, E_CONV), lambda i, j: (i // tiles_per_seq, 0, 0)),
            pl.BlockSpec((CONV_W, E_CONV), lambda i, j: (0, 0)),
        ],
        out_specs=[
            pl.BlockSpec((tm, CB), lambda i, j: (i, jnp.maximum(j - 3, 0))),
            pl.BlockSpec((1, 2 * shift, E_CONV), lambda i, j: (i // tiles_per_seq, 0, 0)),
        ],
        out_shape=[
            jax.ShapeDtypeStruct((rows, N_ACT_BLOCKS * CB), BF16),
            jax.ShapeDtypeStruct((n_seq, 2 * shift, E_CONV), F32),
        ],
        scratch_shapes=[
            pltpu.VMEM((tm, D_MODEL), BF16),
            pltpu.VMEM((tm, E_CONV), F32),
            pltpu.VMEM((pad + tm, E_CONV), F32),
        ],
        compiler_params=_cparams(("arbitrary", "arbitrary")),
        name="inproj",
    )(x2d, g, w_bf, halo, conv_w)


def _memproj_kernel(x_ref, g_ref, w_ref, o_ref):
    h = _rms(x_ref[...], g_ref[...]).astype(BF16)
    o_ref[...] = jnp.dot(h, w_ref[...], preferred_element_type=F32)


def _memproj(mem2d, g, w_bf):
    rows = mem2d.shape[0]
    tm = 512
    return pl.pallas_call(
        _memproj_kernel,
        grid=(rows // tm,),
        in_specs=[
            pl.BlockSpec((tm, D_MODEL), lambda i: (i, 0)),
            pl.BlockSpec((1, D_MODEL), lambda i: (0, 0)),
            pl.BlockSpec((D_MODEL, E_ATTN), lambda i: (0, 0)),
        ],
        out_specs=pl.BlockSpec((tm, E_ATTN), lambda i: (i, 0)),
        out_shape=jax.ShapeDtypeStruct((rows, E_ATTN), F32),
        compiler_params=_cparams(("arbitrary",)),
        name="memproj",
    )(mem2d, g, w_bf)


def _disc_kernel(lr_ref, li_ref, ldt_ref, bre_ref, bim_ref,
                 ar_ref, ai_ref, pr_ref, pi_ref, bbr_ref, bbi_ref, *, n_sq):
    lr = lr_ref[...]
    li = li_ref[...]
    dt = jnp.exp(ldt_ref[...])
    mag = jnp.exp(lr * dt)
    ang = li * dt
    ar = mag * jnp.cos(ang)
    ai = mag * jnp.sin(ang)
    den = lr * lr + li * li
    fr = ((ar - 1.0) * lr + ai * li) / den
    fi = (ai * lr - (ar - 1.0) * li) / den
    bre = bre_ref[...]
    bim = bim_ref[...]
    bbr_ref[...] = fr * bre - fi * bim
    bbi_ref[...] = fr * bim + fi * bre
    ar_ref[...] = ar
    ai_ref[...] = ai
    pr, pi = ar, ai
    for _ in range(n_sq):
        pr, pi = pr * pr - pi * pi, 2.0 * pr * pi
    pr_ref[...] = pr
    pi_ref[...] = pi


def _discretise(lam_re, lam_im, log_dt, b_re, b_im, n_sq):
    rows = N_GROUPS * GROUP
    rep = lambda a: jnp.repeat(a, GROUP, axis=0)
    lr = rep(lam_re)
    li = rep(lam_im)
    ldt = rep(jnp.broadcast_to(log_dt[:, None], (N_GROUPS, P_STATE)))
    bre_t = jnp.transpose(b_re, (0, 2, 1)).reshape(rows, P_STATE)
    bim_t = jnp.transpose(b_im, (0, 2, 1)).reshape(rows, P_STATE)
    sds = jax.ShapeDtypeStruct((rows, P_STATE), F32)
    outs = pl.pallas_call(
        functools.partial(_disc_kernel, n_sq=n_sq),
        out_shape=[sds] * 6,
        name="s5_disc",
    )(lr, li, ldt, bre_t, bim_t)
    ar, ai, pr, pi, bbr_t, bbi_t = outs
    pick = lambda a: a.reshape(N_GROUPS, GROUP, P_STATE)[:, 0, :].reshape(N_BLK, 1, BLK_ST)
    abar = jnp.concatenate([pick(ar), pick(ai)], axis=1)
    apow = jnp.concatenate([pick(pr), pick(pi)], axis=1)
    return abar, apow, bbr_t, bbi_t


def _block_diag_weights(bbr_t, bbi_t, c_re, c_im):
    eye = jnp.eye(GROUPS_PER_BLK, dtype=F32)

    def b_side(a):
        a = a.reshape(N_BLK, GROUPS_PER_BLK, GROUP, P_STATE)
        return jnp.einsum('bgcp,gh->bgchp', a, eye).reshape(N_BLK, BLK_IN, BLK_ST)

    def c_side(a):
        a = a.reshape(N_BLK, GROUPS_PER_BLK, GROUP, P_STATE)
        return jnp.einsum('bgcp,gh->bhpgc', a, eye).reshape(N_BLK, BLK_ST, BLK_IN)

    bblk = jnp.concatenate([b_side(bbr_t), b_side(bbi_t)], axis=2).astype(BF16)
    cblk = jnp.concatenate([c_side(c_re), c_side(-c_im)], axis=1).astype(BF16)
    return bblk, cblk


def _cmul_add(ar, ai, sr, si, xr, xi):
    return ar * sr - ai * si + xr, ar * si + ai * sr + xi


def _ssm_prompt_kernel(u_ref, sz_ref, bblk_ref, cblk_ref, abar_ref, apow_ref, d_ref,
                       ys_ref, sre_ref, sim_ref,
                       x_scr, s_scr, y_scr, ynat_scr, carry_scr, *, tq):
    i = pl.program_id(1)
    seg_len = tq // SEGS
    n_half = BLK_ST // SCAN_W

    @pl.when(i == 0)
    def _():
        carry_scr[...] = jnp.zeros_like(carry_scr)

    u = u_ref[...]
    r = lax.broadcasted_iota(jnp.int32, (tq, tq), 0)
    c = lax.broadcasted_iota(jnp.int32, (tq, tq), 1)
    perm = jnp.where(c == (r % SEGS) * seg_len + r // SEGS, 1.0, 0.0).astype(BF16)
    u_perm = jnp.dot(perm, u, preferred_element_type=F32).astype(BF16)
    row_id = lax.broadcasted_iota(jnp.int32, (SEGS, SCAN_W), 0)

    for blk in range(N_BLK):
        x_scr[...] = jnp.dot(u_perm[:, blk * BLK_IN:(blk + 1) * BLK_IN], bblk_ref[blk],
                             preferred_element_type=F32)
        for half in range(n_half):
            lo = half * SCAN_W
            re_cols = slice(lo, lo + SCAN_W)
            im_cols = slice(BLK_ST + lo, BLK_ST + lo + SCAN_W)
            ar = jnp.broadcast_to(abar_ref[blk, 0:1, lo:lo + SCAN_W], (SEGS, SCAN_W))
            ai = jnp.broadcast_to(abar_ref[blk, 1:2, lo:lo + SCAN_W], (SEGS, SCAN_W))

            def local_end(j, carry, re_cols=re_cols, im_cols=im_cols, ar=ar, ai=ai):
                row = pl.multiple_of(j * SEGS, SEGS)
                xr = x_scr[pl.ds(row, SEGS), re_cols]
                xi = x_scr[pl.ds(row, SEGS), im_cols]
                return _cmul_add(ar, ai, carry[0], carry[1], xr, xi)

            zero = jnp.zeros((SEGS, SCAN_W), F32)
            er, ei = lax.fori_loop(0, seg_len, local_end, (zero, zero), unroll=4)

            pr = apow_ref[blk, 0:1, lo:lo + SCAN_W]
            pi = apow_ref[blk, 1:2, lo:lo + SCAN_W]
            prev_r = carry_scr[blk, 0:1, lo:lo + SCAN_W]
            prev_i = carry_scr[blk, 1:2, lo:lo + SCAN_W]
            init_r = zero
            init_i = zero
            for k in range(SEGS):
                init_r = jnp.where(row_id == k, prev_r, init_r)
                init_i = jnp.where(row_id == k, prev_i, init_i)
                prev_r, prev_i = _cmul_add(pr, pi, prev_r, prev_i,
                                           er[k:k + 1, :], ei[k:k + 1, :])
            carry_scr[blk, 0:1, lo:lo + SCAN_W] = prev_r
            carry_scr[blk, 1:2, lo:lo + SCAN_W] = prev_i

            def full_scan(j, carry, re_cols=re_cols, im_cols=im_cols, ar=ar, ai=ai):
                row = pl.multiple_of(j * SEGS, SEGS)
                xr = x_scr[pl.ds(row, SEGS), re_cols]
                xi = x_scr[pl.ds(row, SEGS), im_cols]
                sr, si = _cmul_add(ar, ai, carry[0], carry[1], xr, xi)
                s_scr[pl.ds(row, SEGS), re_cols] = sr
                s_scr[pl.ds(row, SEGS), im_cols] = si
                return sr, si

            lax.fori_loop(0, seg_len, full_scan, (init_r, init_i), unroll=4)

        y = jnp.dot(s_scr[...].astype(BF16), cblk_ref[blk], preferred_element_type=F32)
        for sl in range(BLK_IN // 128):
            y_scr[blk * (BLK_IN // 128) + sl] = y[:, sl * 128:(sl + 1) * 128]

    for slab in range(E_SSM // 128):
        for k in range(SEGS):
            ynat_scr[k * seg_len:(k + 1) * seg_len, slab * 128:(slab + 1) * 128] = (
                y_scr[slab, pl.ds(k, seg_len, stride=SEGS), :])

    yy = (ynat_scr[...] + d_ref[...] * u.astype(F32)) * sz_ref[...].astype(F32)
    ys_ref[...] = jax.nn.gelu(yy).astype(BF16)
    for blk in range(N_BLK):
        sre_ref[0, :, blk * BLK_ST:(blk + 1) * BLK_ST] = carry_scr[blk, 0:1, :]
        sim_ref[0, :, blk * BLK_ST:(blk + 1) * BLK_ST] = carry_scr[blk, 1:2, :]


def _ssm_prompt(act, bblk, cblk, abar, apow, d_skip, *, tq):
    tiles = SEQ // tq
    n_half = BLK_ST // SCAN_W
    st = jax.ShapeDtypeStruct((BATCH, 1, N_GROUPS * P_STATE), F32)
    const3 = lambda b, i: (0, 0, 0)
    return pl.pallas_call(
        functools.partial(_ssm_prompt_kernel, tq=tq),
        grid=(BATCH, tiles),
        in_specs=[
            pl.BlockSpec((tq, E_SSM), lambda b, i: (b * tiles + i, 1)),
            pl.BlockSpec((tq, E_SSM), lambda b, i: (b * tiles + i, 2)),
            pl.BlockSpec((N_BLK, BLK_IN, 2 * BLK_ST), const3),
            pl.BlockSpec((N_BLK, 2 * BLK_ST, BLK_IN), const3),
            pl.BlockSpec((N_BLK, 2, BLK_ST), const3),
            pl.BlockSpec((N_BLK, 2, BLK_ST), const3),
            pl.BlockSpec((1, E_SSM), lambda b, i: (0, 0)),
        ],
        out_specs=[
            pl.BlockSpec((tq, E_SSM), lambda b, i: (b * tiles + i, 0)),
            pl.BlockSpec((1, 1, N_GROUPS * P_STATE), lambda b, i: (b, 0, 0)),
            pl.BlockSpec((1, 1, N_GROUPS * P_STATE), lambda b, i: (b, 0, 0)),
        ],
        out_shape=[jax.ShapeDtypeStruct((BATCH * SEQ, E_SSM), BF16), st, st],
        scratch_shapes=[
            pltpu.VMEM((tq, 2 * BLK_ST), F32),
            pltpu.VMEM((tq, 2 * BLK_ST), F32),
            pltpu.VMEM((E_SSM // 128, tq, 128), F32),
            pltpu.VMEM((tq, E_SSM), F32),
            pltpu.VMEM((N_BLK, 2, BLK_ST), F32),
        ],
        compiler_params=_cparams(("arbitrary", "arbitrary")),
        name="ssm_prompt",
    )(act, act, bblk, cblk, abar, apow, d_skip)


def _ssm_sample_kernel(u_ref, sz_ref, bblk_ref, cblk_ref, abar_ref, d_ref, s0r_ref, s0i_ref,
                       ys_ref, sre_ref, sim_ref, x_scr, s_scr):
    n_half = BLK_ST // SCAN_W
    rows = DEC_SEQ * DEC_BATCH
    u = u_ref[...]
    ys = []
    for blk in range(N_BLK):
        x = jnp.dot(u[:, blk * BLK_IN:(blk + 1) * BLK_IN], bblk_ref[blk],
                    preferred_element_type=F32)
        for cch in range(2 * n_half):
            x_scr[cch] = x[:, cch * SCAN_W:(cch + 1) * SCAN_W]
        for half in range(n_half):
            lo = half * SCAN_W
            col = blk * BLK_ST + lo
            ar = jnp.broadcast_to(abar_ref[blk, 0:1, lo:lo + SCAN_W], (8, SCAN_W))
            ai = jnp.broadcast_to(abar_ref[blk, 1:2, lo:lo + SCAN_W], (8, SCAN_W))

            def body(rc, carry, half=half, col=col, ar=ar, ai=ai):
                r0 = pl.multiple_of(rc * 8, 8)
                sr = s0r_ref[pl.ds(r0, 8), col:col + SCAN_W]
                si = s0i_ref[pl.ds(r0, 8), col:col + SCAN_W]
                for t in range(DEC_SEQ):
                    row = pl.multiple_of(t * DEC_BATCH + r0, 8)
                    xr = x_scr[half, pl.ds(row, 8), :]
                    xi = x_scr[n_half + half, pl.ds(row, 8), :]
                    sr, si = _cmul_add(ar, ai, sr, si, xr, xi)
                    s_scr[half, pl.ds(row, 8), :] = sr
                    s_scr[n_half + half, pl.ds(row, 8), :] = si
                sre_ref[pl.ds(r0, 8), col:col + SCAN_W] = sr
                sim_ref[pl.ds(r0, 8), col:col + SCAN_W] = si
                return carry

            lax.fori_loop(0, DEC_BATCH // 8, body, 0)

        y = jnp.zeros((rows, BLK_IN), F32)
        for cch in range(2 * n_half):
            y = y + jnp.dot(s_scr[cch].astype(BF16),
                            cblk_ref[blk, cch * SCAN_W:(cch + 1) * SCAN_W, :],
                            preferred_element_type=F32)
        ys.append(y)
    y_all = jnp.concatenate(ys, axis=1)
    yy = (y_all + d_ref[...] * u.astype(F32)) * sz_ref[...].astype(F32)
    ys_ref[...] = jax.nn.gelu(yy).astype(BF16)


def _ssm_sample(act, bblk, cblk, abar, d_skip, s0r, s0i):
    rows = DEC_SEQ * DEC_BATCH
    n_half = BLK_ST // SCAN_W
    st = jax.ShapeDtypeStruct((DEC_BATCH, N_GROUPS * P_STATE), F32)
    full = lambda shape: pl.BlockSpec(shape, lambda i: tuple(0 for _ in shape))
    return pl.pallas_call(
        _ssm_sample_kernel,
        grid=(1,),
        in_specs=[
            pl.BlockSpec((rows, E_SSM), lambda i: (0, 1)),
            pl.BlockSpec((rows, E_SSM), lambda i: (0, 2)),
            full((N_BLK, BLK_IN, 2 * BLK_ST)),
            full((N_BLK, 2 * BLK_ST, BLK_IN)),
            full((N_BLK, 2, BLK_ST)),
            full((1, E_SSM)),
            full((DEC_BATCH, N_GROUPS * P_STATE)),
            full((DEC_BATCH, N_GROUPS * P_STATE)),
        ],
        out_specs=[full((rows, E_SSM)), full((DEC_BATCH, N_GROUPS * P_STATE)),
                   full((DEC_BATCH, N_GROUPS * P_STATE))],
        out_shape=[jax.ShapeDtypeStruct((rows, E_SSM), BF16), st, st],
        scratch_shapes=[
            pltpu.VMEM((2 * n_half, rows, SCAN_W), F32),
            pltpu.VMEM((2 * n_half, rows, SCAN_W), F32),
        ],
        compiler_params=_cparams(("arbitrary",)),
        name="ssm_sample",
    )(act, act, bblk, cblk, abar, d_skip, s0r, s0i)


def _softmax_rows(s):
    m = jnp.max(s, axis=-1, keepdims=True)
    e = jnp.exp(s - m)
    return e / jnp.sum(e, axis=-1, keepdims=True)


def _attn_prompt_kernel(q_ref, az_ref, k_ref, v_ref, o_ref):
    for h in range(N_HEADS):
        cols = slice(h * HEAD_DIM, (h + 1) * HEAD_DIM)
        kh = k_ref[:, cols].astype(BF16)
        vh = v_ref[:, cols].astype(BF16)
        s = lax.dot_general(q_ref[:, cols], kh, (((1,), (1,)), ((), ())),
                            preferred_element_type=F32)
        p = _softmax_rows(s).astype(BF16)
        o = jnp.dot(p, vh, preferred_element_type=F32)
        o_ref[:, cols] = (o * az_ref[:, cols].astype(F32)).astype(BF16)


def _attn_prompt(act, mk2d, mv2d, *, tr):
    tiles = SEQ // tr
    return pl.pallas_call(
        _attn_prompt_kernel,
        grid=(BATCH, tiles),
        in_specs=[
            pl.BlockSpec((tr, E_ATTN), lambda b, i: (b * tiles + i, 3)),
            pl.BlockSpec((tr, E_ATTN), lambda b, i: (b * tiles + i, 4)),
            pl.BlockSpec((MEM_LEN, E_ATTN), lambda b, i: (b, 0)),
            pl.BlockSpec((MEM_LEN, E_ATTN), lambda b, i: (b, 0)),
        ],
        out_specs=pl.BlockSpec((tr, E_ATTN), lambda b, i: (b * tiles + i, 0)),
        out_shape=jax.ShapeDtypeStruct((BATCH * SEQ, E_ATTN), BF16),
        compiler_params=_cparams(("arbitrary", "arbitrary")),
        name="attn_prompt",
    )(act, act, mk2d, mv2d)


def _attn_sample_kernel(q_ref, az_ref, k_ref, v_ref, o_ref, *, bt):
    n_kv = bt * MEM_LEN * N_HEADS
    k = k_ref[...].reshape(n_kv, HEAD_DIM).astype(BF16)
    v = v_ref[...].reshape(n_kv, HEAD_DIM).astype(BF16)
    rows = DEC_SEQ * bt
    q = jnp.concatenate(
        [q_ref[:, :, h * HEAD_DIM:(h + 1) * HEAD_DIM].reshape(rows, HEAD_DIM)
         for h in range(N_HEADS)], axis=0)
    s = lax.dot_general(q, k, (((1,), (1,)), ((), ())), preferred_element_type=F32)
    row = lax.broadcasted_iota(jnp.int32, s.shape, 0)
    col = lax.broadcasted_iota(jnp.int32, s.shape, 1)
    same = ((row // rows == col % N_HEADS)
            & (row % bt == col // (MEM_LEN * N_HEADS)))
    s = jnp.where(same, s, -1e30)
    p = _softmax_rows(s).astype(BF16)
    o = jnp.dot(p, v, preferred_element_type=F32)
    for h in range(N_HEADS):
        cols = slice(h * HEAD_DIM, (h + 1) * HEAD_DIM)
        az = az_ref[:, :, cols].reshape(rows, HEAD_DIM).astype(F32)
        o_ref[:, :, cols] = (o[h * rows:(h + 1) * rows] * az).astype(BF16).reshape(
            DEC_SEQ, bt, HEAD_DIM)


def _attn_sample(act3, k4, v4, *, bt):
    kv_spec = pl.BlockSpec((bt, MEM_LEN, N_HEADS, HEAD_DIM), lambda b: (b, 0, 0, 0))
    return pl.pallas_call(
        functools.partial(_attn_sample_kernel, bt=bt),
        grid=(DEC_BATCH // bt,),
        in_specs=[
            pl.BlockSpec((DEC_SEQ, bt, E_ATTN), lambda b: (0, b, 3)),
            pl.BlockSpec((DEC_SEQ, bt, E_ATTN), lambda b: (0, b, 4)),
            kv_spec, kv_spec,
        ],
        out_specs=pl.BlockSpec((DEC_SEQ, bt, E_ATTN), lambda b: (0, b, 0)),
        out_shape=jax.ShapeDtypeStruct((DEC_SEQ, DEC_BATCH, E_ATTN), BF16),
        compiler_params=_cparams(("arbitrary",)),
        name="attn_sample",
    )(act3, act3, k4, v4)


def _merge_kernel(gated_ref, ys_ref, oz_ref, g0_ref, g1_ref, g2_ref,
                  wco_ref, wga_ref, wgb_ref, wao_ref, o_ref):
    dot = functools.partial(jnp.dot, preferred_element_type=F32)
    conv_out = dot(gated_ref[...], wco_ref[...])
    ys = ys_ref[...]
    ssm_out = dot(ys, wga_ref[...]) * jax.nn.sigmoid(dot(ys, wgb_ref[...]))
    attn_out = dot(oz_ref[...], wao_ref[...])
    merged = (g0_ref[...].astype(F32) * conv_out + g1_ref[...].astype(F32) * ssm_out
              + g2_ref[...].astype(F32) * attn_out)
    o_ref[...] = merged.astype(BF16)


def _merge(act, ys, oz, wco, wga, wgb, wao, *, tm):
    rows = act.shape[0]
    tn = 1024
    nn = D_MODEL // tn
    wspec = pl.BlockSpec((E_CONV, tn), lambda i, n: (0, n))
    return pl.pallas_call(
        _merge_kernel,
        grid=(rows // tm, nn),
        in_specs=[
            pl.BlockSpec((tm, CB), lambda i, n: (i, 0)),
            pl.BlockSpec((tm, E_SSM), lambda i, n: (i, 0)),
            pl.BlockSpec((tm, E_ATTN), lambda i, n: (i, 0)),
            pl.BlockSpec((tm, tn), lambda i, n: (i, 5 + n)),
            pl.BlockSpec((tm, tn), lambda i, n: (i, 5 + nn + n)),
            pl.BlockSpec((tm, tn), lambda i, n: (i, 5 + 2 * nn + n)),
            wspec, wspec, wspec, wspec,
        ],
        out_specs=pl.BlockSpec((tm, tn), lambda i, n: (i, n)),
        out_shape=jax.ShapeDtypeStruct((rows, D_MODEL), BF16),
        compiler_params=_cparams(("arbitrary", "arbitrary")),
        name="merge",
    )(act, ys, oz, act, act, act, wco, wga, wgb, wao)


def _out_kernel(x_ref, m_ref, w_ref, g_ref, o_ref):
    y = x_ref[...] + jnp.dot(m_ref[...], w_ref[...], preferred_element_type=F32)
    o_ref[...] = _rms(y, g_ref[...])


def _outproj(x2d, merged, w_bf, g, *, tm):
    rows = x2d.shape[0]
    return pl.pallas_call(
        _out_kernel,
        grid=(rows // tm,),
        in_specs=[
            pl.BlockSpec((tm, D_MODEL), lambda i: (i, 0)),
            pl.BlockSpec((tm, D_MODEL), lambda i: (i, 0)),
            pl.BlockSpec((D_MODEL, D_MODEL), lambda i: (0, 0)),
            pl.BlockSpec((1, D_MODEL), lambda i: (0, 0)),
        ],
        out_specs=pl.BlockSpec((tm, D_MODEL), lambda i: (i, 0)),
        out_shape=jax.ShapeDtypeStruct((rows, D_MODEL), F32),
        compiler_params=_cparams(("arbitrary",)),
        name="outproj",
    )(x2d, merged, w_bf, g)


def kernel(x_prompt, x_sample, mem_prompt, cache_mem_k, cache_mem_v, state_conv, state_ssm_re, state_ssm_im, norm_g, mem_norm_g, w_in, conv_w, w_conv_out, ssm_lambda_re, ssm_lambda_im, ssm_log_dt, ssm_b_re, ssm_b_im, ssm_c_re, ssm_c_im, ssm_d, w_glu_a, w_glu_b, w_mem_k, w_mem_v, w_attn_out, w_out, final_norm_g):
    l = 0
    tq = 512
    bf = lambda w: w.astype(BF16)
    w_in_bf = bf(w_in[l])
    wco, wga, wgb, wao, wo = (bf(w_conv_out[l]), bf(w_glu_a[l]), bf(w_glu_b[l]),
                              bf(w_attn_out[l]), bf(w_out[l]))
    g_in = norm_g[l][None, :]
    g_fin = final_norm_g[None, :]
    d_skip = ssm_d[l][None, :]

    n_sq = int(math.log2(tq // SEGS))
    abar, apow, bbr_t, bbi_t = _discretise(ssm_lambda_re[l], ssm_lambda_im[l], ssm_log_dt[l],
                                           ssm_b_re[l], ssm_b_im[l], n_sq)
    bblk, cblk = _block_diag_weights(bbr_t, bbi_t, ssm_c_re[l], ssm_c_im[l])

    xp = x_prompt.reshape(BATCH * SEQ, D_MODEL)
    tm_p = 1024
    act_p, nconv_p = _inproj(xp, g_in, w_in_bf, jnp.zeros((BATCH, 2, E_CONV), F32), conv_w[l],
                             tm=tm_p, shift=1, pad=8, tiles_per_seq=SEQ // tm_p)
    mem2d = mem_prompt.reshape(BATCH * MEM_LEN, D_MODEL)
    g_mem = mem_norm_g[l][None, :]
    mk = _memproj(mem2d, g_mem, bf(w_mem_k[l]))
    mv = _memproj(mem2d, g_mem, bf(w_mem_v[l]))
    ys_p, sre_p, sim_p = _ssm_prompt(act_p, bblk, cblk, abar, apow, d_skip, tq=tq)
    oz_p = _attn_prompt(act_p, mk, mv, tr=512)
    merged_p = _merge(act_p, ys_p, oz_p, wco, wga, wgb, wao, tm=512)
    y_p = _outproj(xp, merged_p, wo, g_fin, tm=512)

    rows_s = DEC_SEQ * DEC_BATCH
    xs = jnp.transpose(x_sample, (1, 0, 2)).reshape(rows_s, D_MODEL)
    halo_s = jnp.transpose(state_conv[l], (1, 0, 2)).reshape(1, 2 * DEC_BATCH, E_CONV)
    act_s, nconv_s = _inproj(xs, g_in, w_in_bf, halo_s, conv_w[l],
                             tm=rows_s, shift=DEC_BATCH, pad=2 * DEC_BATCH, tiles_per_seq=1)
    s0r = state_ssm_re[l].reshape(DEC_BATCH, N_GROUPS * P_STATE)
    s0i = state_ssm_im[l].reshape(DEC_BATCH, N_GROUPS * P_STATE)
    ys_s, sre_s, sim_s = _ssm_sample(act_s, bblk, cblk, abar, d_skip, s0r, s0i)
    k3 = cache_mem_k[l]
    v3 = cache_mem_v[l]
    oz_s = _attn_sample(act_s.reshape(DEC_SEQ, DEC_BATCH, N_ACT_BLOCKS * CB), k3, v3, bt=8)
    merged_s = _merge(act_s, ys_s, oz_s.reshape(rows_s, E_ATTN), wco, wga, wgb, wao, tm=512)
    y_s = _outproj(xs, merged_s, wo, g_fin, tm=512)

    y_prompt = y_p.reshape(BATCH, SEQ, D_MODEL)
    y_sample = jnp.transpose(y_s.reshape(DEC_SEQ, DEC_BATCH, D_MODEL), (1, 0, 2))
    st_shape_p = (1, BATCH, N_GROUPS, P_STATE)
    st_shape_s = (1, DEC_BATCH, N_GROUPS, P_STATE)
    new_conv_s = jnp.transpose(nconv_s.reshape(2, DEC_BATCH, E_CONV), (1, 0, 2))
    return (y_prompt, y_sample,
            mk.reshape(1, BATCH, MEM_LEN, N_HEADS, HEAD_DIM),
            mv.reshape(1, BATCH, MEM_LEN, N_HEADS, HEAD_DIM),
            nconv_p[None],
            sre_p.reshape(st_shape_p), sim_p.reshape(st_shape_p),
            new_conv_s[None],
            sre_s.reshape(st_shape_s), sim_s.reshape(st_shape_s))
```

```python
import functools
import math

import jax
import jax.numpy as jnp
from jax import lax
from jax.experimental import pallas as pl
from jax.experimental.pallas import tpu as pltpu

D_MODEL = 2048
BATCH = 4
SEQ = 2048
DEC_BATCH = 128
DEC_SEQ = 4
E_CONV = 1024
CONV_W = 3
E_SSM = 1024
GROUP = 16
N_GROUPS = 64
P_STATE = 64
N_HEADS = 4
HEAD_DIM = 256
E_ATTN = 1024
MEM_LEN = 256
N_IN = 14336
EPS = 1e-6

F32 = jnp.float32
BF16 = jnp.bfloat16

CB = 1024
N_ACT_BLOCKS = 11
GROUPS_PER_BLK = 16
N_BLK = N_GROUPS // GROUPS_PER_BLK
BLK_IN = GROUPS_PER_BLK * GROUP
BLK_ST = GROUPS_PER_BLK * P_STATE
SCAN_W = 512
SEGS = 8
VMEM_LIMIT = 56 * 1024 * 1024


def _cparams(sem):
    return pltpu.CompilerParams(dimension_semantics=sem, vmem_limit_bytes=VMEM_LIMIT)


def _rms(x, g):
    ms = jnp.mean(x * x, axis=-1, keepdims=True)
    return x * lax.rsqrt(ms + EPS) * g


def _inproj_kernel(x_ref, g_ref, w_ref, halo_ref, cw_ref, act_ref, nconv_ref,
                   h_scr, cb_scr, vext_scr, *, tm, shift, pad, tiles_per_seq):
    i = pl.program_id(0)
    j = pl.program_id(1)

    @pl.when(j == 0)
    def _():
        h_scr[...] = _rms(x_ref[...], g_ref[...]).astype(BF16)

    def proj():
        return jnp.dot(h_scr[...], w_ref[...], preferred_element_type=F32)

    @pl.when(j == 0)
    def _():
        cb_scr[...] = proj()

    @pl.when(j == 1)
    def _():
        vext_scr[pl.ds(pad, tm), :] = proj()

    @pl.when(j == 2)
    def _():
        vext_scr[pl.ds(pad, tm), :] = vext_scr[pl.ds(pad, tm), :] * proj()

        @pl.when(i % tiles_per_seq == 0)
        def _():
            vext_scr[pl.ds(pad - 2 * shift, 2 * shift), :] = halo_ref[0]

    @pl.when(j == 3)
    def _():
        conv = (cw_ref[0:1, :] * vext_scr[pl.ds(pad - 2 * shift, tm), :]
                + cw_ref[1:2, :] * vext_scr[pl.ds(pad - shift, tm), :]
                + cw_ref[2:3, :] * vext_scr[pl.ds(pad, tm), :])
        act_ref[...] = (cb_scr[...] * conv * jax.nn.silu(proj())).astype(BF16)
        tail = vext_scr[pl.ds(pad + tm - 2 * shift, 2 * shift), :]
        nconv_ref[0] = tail
        vext_scr[pl.ds(pad - 2 * shift, 2 * shift), :] = tail

    @pl.when(j == 4)
    def _():
        act_ref[...] = proj().astype(BF16)

    @pl.when(j == 6)
    def _():
        act_ref[...] = (proj() * (HEAD_DIM ** -0.5)).astype(BF16)

    @pl.when((j == 5) | (j == 7))
    def _():
        act_ref[...] = jax.nn.silu(proj()).astype(BF16)

    @pl.when(j >= 8)
    def _():
        act_ref[...] = jax.nn.sigmoid(proj()).astype(BF16)


def _inproj(x2d, g, w_bf, halo, conv_w, *, tm, shift, pad, tiles_per_seq):
    rows = x2d.shape[0]
    n_seq = rows // (tm * tiles_per_seq)
    kern = functools.partial(_inproj_kernel, tm=tm, shift=shift, pad=pad,
                             tiles_per_seq=tiles_per_seq)
    return pl.pallas_call(
        kern,
        grid=(rows // tm, N_IN // CB),
        in_specs=[
            pl.BlockSpec((tm, D_MODEL), lambda i, j: (i, 0)),
            pl.BlockSpec((1, D_MODEL), lambda i, j: (0, 0)),
            pl.BlockSpec((D_MODEL, CB), lambda i, j: (0, j)),
            pl.BlockSpec((1, 2 * shift, E_CONV), lambda i, j: (i // tiles_per_seq, 0, 0)),
            pl.BlockSpec((CONV_W, E_CONV), lambda i, j: (0, 0)),
        ],
        out_specs=[
            pl.BlockSpec((tm, CB), lambda i, j: (i, jnp.maximum(j - 3, 0))),
            pl.BlockSpec((1, 2 * shift, E_CONV), lambda i, j: (i // tiles_per_seq, 0, 0)),
        ],
        out_shape=[
            jax.ShapeDtypeStruct((rows, N_ACT_BLOCKS * CB), BF16),
            jax.ShapeDtypeStruct((n_seq, 2 * shift, E_CONV), F32),
        ],
        scratch_shapes=[
            pltpu.VMEM((tm, D_MODEL), BF16),
            pltpu.VMEM((tm, E_CONV), F32),
            pltpu.VMEM((pad + tm, E_CONV), F32),
        ],
        compiler_params=_cparams(("arbitrary", "arbitrary")),
        name="inproj",
    )(x2d, g, w_bf, halo, conv_w)


def _memproj_kernel(x_ref, g_ref, w_ref, o_ref):
    h = _rms(x_ref[...], g_ref[...]).astype(BF16)
    o_ref[...] = jnp.dot(h, w_ref[...].astype(BF16), preferred_element_type=F32)


def _memproj(mem2d, g, w_bf):
    rows = mem2d.shape[0]
    tm = 512
    return pl.pallas_call(
        _memproj_kernel,
        grid=(rows // tm,),
        in_specs=[
            pl.BlockSpec((tm, D_MODEL), lambda i: (i, 0)),
            pl.BlockSpec((1, D_MODEL), lambda i: (0, 0)),
            pl.BlockSpec((D_MODEL, E_ATTN), lambda i: (0, 0)),
        ],
        out_specs=pl.BlockSpec((tm, E_ATTN), lambda i: (i, 0)),
        out_shape=jax.ShapeDtypeStruct((rows, E_ATTN), F32),
        compiler_params=_cparams(("arbitrary",)),
        name="memproj",
    )(mem2d, g, w_bf)


def _disc_kernel(lr_ref, li_ref, ldt_ref, bre_ref, bim_ref,
                 ar_ref, ai_ref, pr_ref, pi_ref, bbr_ref, bbi_ref, *, n_sq):
    lr = lr_ref[...]
    li = li_ref[...]
    dt = jnp.exp(ldt_ref[...])
    mag = jnp.exp(lr * dt)
    ang = li * dt
    ar = mag * jnp.cos(ang)
    ai = mag * jnp.sin(ang)
    den = lr * lr + li * li
    fr = ((ar - 1.0) * lr + ai * li) / den
    fi = (ai * lr - (ar - 1.0) * li) / den
    bre = bre_ref[...]
    bim = bim_ref[...]
    bbr_ref[...] = fr * bre - fi * bim
    bbi_ref[...] = fr * bim + fi * bre
    ar_ref[...] = ar
    ai_ref[...] = ai
    pr, pi = ar, ai
    for _ in range(n_sq):
        pr, pi = pr * pr - pi * pi, 2.0 * pr * pi
    pr_ref[...] = pr
    pi_ref[...] = pi


def _discretise(lam_re, lam_im, log_dt, b_re, b_im, n_sq):
    rows = N_GROUPS * GROUP
    rep = lambda a: jnp.repeat(a, GROUP, axis=0)
    lr = rep(lam_re)
    li = rep(lam_im)
    ldt = rep(jnp.broadcast_to(log_dt[:, None], (N_GROUPS, P_STATE)))
    bre_t = jnp.transpose(b_re, (0, 2, 1)).reshape(rows, P_STATE)
    bim_t = jnp.transpose(b_im, (0, 2, 1)).reshape(rows, P_STATE)
    sds = jax.ShapeDtypeStruct((rows, P_STATE), F32)
    outs = pl.pallas_call(
        functools.partial(_disc_kernel, n_sq=n_sq),
        out_shape=[sds] * 6,
        name="s5_disc",
    )(lr, li, ldt, bre_t, bim_t)
    ar, ai, pr, pi, bbr_t, bbi_t = outs
    pick = lambda a: a.reshape(N_GROUPS, GROUP, P_STATE)[:, 0, :].reshape(N_BLK, 1, BLK_ST)
    abar = jnp.concatenate([pick(ar), pick(ai)], axis=1)
    apow = jnp.concatenate([pick(pr), pick(pi)], axis=1)
    return abar, apow, bbr_t, bbi_t


def _block_diag_weights(bbr_t, bbi_t, c_re, c_im):
    eye = jnp.eye(GROUPS_PER_BLK, dtype=F32)

    def b_side(a):
        a = a.reshape(N_BLK, GROUPS_PER_BLK, GROUP, P_STATE)
        return jnp.einsum('bgcp,gh->bgchp', a, eye).reshape(N_BLK, BLK_IN, BLK_ST)

    def c_side(a):
        a = a.reshape(N_BLK, GROUPS_PER_BLK, GROUP, P_STATE)
        return jnp.einsum('bgcp,gh->bhpgc', a, eye).reshape(N_BLK, BLK_ST, BLK_IN)

    bblk = jnp.concatenate([b_side(bbr_t), b_side(bbi_t)], axis=2).astype(BF16)
    cblk = jnp.concatenate([c_side(c_re), c_side(-c_im)], axis=1).astype(BF16)
    return bblk, cblk


def _cmul_add(ar, ai, sr, si, xr, xi):
    return ar * sr - ai * si + xr, ar * si + ai * sr + xi


def _ssm_prompt_kernel(u_ref, sz_ref, bblk_ref, cblk_ref, abar_ref, apow_ref, d_ref,
                       ys_ref, sre_ref, sim_ref,
                       x_scr, s_scr, y_scr, ynat_scr, carry_scr, *, tq):
    i = pl.program_id(1)
    seg_len = tq // SEGS
    n_half = BLK_ST // SCAN_W

    @pl.when(i == 0)
    def _():
        carry_scr[...] = jnp.zeros_like(carry_scr)

    u = u_ref[...]
    r = lax.broadcasted_iota(jnp.int32, (tq, tq), 0)
    c = lax.broadcasted_iota(jnp.int32, (tq, tq), 1)
    perm = jnp.where(c == (r % SEGS) * seg_len + r // SEGS, 1.0, 0.0).astype(BF16)
    u_perm = jnp.dot(perm, u, preferred_element_type=F32).astype(BF16)
    row_id = lax.broadcasted_iota(jnp.int32, (SEGS, SCAN_W), 0)

    for blk in range(N_BLK):
        par = blk % 2
        x_scr[par] = jnp.dot(u_perm[:, blk * BLK_IN:(blk + 1) * BLK_IN], bblk_ref[blk],
                             preferred_element_type=F32)
        for half in range(n_half):
            lo = half * SCAN_W
            re_cols = slice(lo, lo + SCAN_W)
            im_cols = slice(BLK_ST + lo, BLK_ST + lo + SCAN_W)
            ar = jnp.broadcast_to(abar_ref[blk, 0:1, lo:lo + SCAN_W], (SEGS, SCAN_W))
            ai = jnp.broadcast_to(abar_ref[blk, 1:2, lo:lo + SCAN_W], (SEGS, SCAN_W))

            def local_end(j, carry, par=par, re_cols=re_cols, im_cols=im_cols, ar=ar, ai=ai):
                row = pl.multiple_of(j * SEGS, SEGS)
                xr = x_scr[par, pl.ds(row, SEGS), re_cols]
                xi = x_scr[par, pl.ds(row, SEGS), im_cols]
                return _cmul_add(ar, ai, carry[0], carry[1], xr, xi)

            zero = jnp.zeros((SEGS, SCAN_W), F32)
            er, ei = lax.fori_loop(0, seg_len, local_end, (zero, zero), unroll=True)

            pr = apow_ref[blk, 0:1, lo:lo + SCAN_W]
            pi = apow_ref[blk, 1:2, lo:lo + SCAN_W]
            prev_r = carry_scr[blk, 0:1, lo:lo + SCAN_W]
            prev_i = carry_scr[blk, 1:2, lo:lo + SCAN_W]
            init_r = zero
            init_i = zero
            for k in range(SEGS):
                init_r = jnp.where(row_id == k, prev_r, init_r)
                init_i = jnp.where(row_id == k, prev_i, init_i)
                prev_r, prev_i = _cmul_add(pr, pi, prev_r, prev_i,
                                           er[k:k + 1, :], ei[k:k + 1, :])
            carry_scr[blk, 0:1, lo:lo + SCAN_W] = prev_r
            carry_scr[blk, 1:2, lo:lo + SCAN_W] = prev_i

            def full_scan(j, carry, par=par, re_cols=re_cols, im_cols=im_cols, ar=ar, ai=ai):
                row = pl.multiple_of(j * SEGS, SEGS)
                xr = x_scr[par, pl.ds(row, SEGS), re_cols]
                xi = x_scr[par, pl.ds(row, SEGS), im_cols]
                sr, si = _cmul_add(ar, ai, carry[0], carry[1], xr, xi)
                s_scr[par, pl.ds(row, SEGS), re_cols] = sr
                s_scr[par, pl.ds(row, SEGS), im_cols] = si
                return sr, si

            lax.fori_loop(0, seg_len, full_scan, (init_r, init_i), unroll=True)

        y = jnp.dot(s_scr[par].astype(BF16), cblk_ref[blk], preferred_element_type=F32)
        for sl in range(BLK_IN // 128):
            y_scr[blk * (BLK_IN // 128) + sl] = y[:, sl * 128:(sl + 1) * 128]

    for slab in range(E_SSM // 128):
        for k in range(SEGS):
            ynat_scr[k * seg_len:(k + 1) * seg_len, slab * 128:(slab + 1) * 128] = (
                y_scr[slab, pl.ds(k, seg_len, stride=SEGS), :])

    yy = (ynat_scr[...] + d_ref[...] * u.astype(F32)) * sz_ref[...].astype(F32)
    ys_ref[...] = jax.nn.gelu(yy).astype(BF16)
    for blk in range(N_BLK):
        sre_ref[0, :, blk * BLK_ST:(blk + 1) * BLK_ST] = carry_scr[blk, 0:1, :]
        sim_ref[0, :, blk * BLK_ST:(blk + 1) * BLK_ST] = carry_scr[blk, 1:2, :]


def _ssm_prompt(act, bblk, cblk, abar, apow, d_skip, *, tq):
    tiles = SEQ // tq
    n_half = BLK_ST // SCAN_W
    st = jax.ShapeDtypeStruct((BATCH, 1, N_GROUPS * P_STATE), F32)
    const3 = lambda b, i: (0, 0, 0)
    return pl.pallas_call(
        functools.partial(_ssm_prompt_kernel, tq=tq),
        grid=(BATCH, tiles),
        in_specs=[
            pl.BlockSpec((tq, E_SSM), lambda b, i: (b * tiles + i, 1)),
            pl.BlockSpec((tq, E_SSM), lambda b, i: (b * tiles + i, 2)),
            pl.BlockSpec((N_BLK, BLK_IN, 2 * BLK_ST), const3),
            pl.BlockSpec((N_BLK, 2 * BLK_ST, BLK_IN), const3),
            pl.BlockSpec((N_BLK, 2, BLK_ST), const3),
            pl.BlockSpec((N_BLK, 2, BLK_ST), const3),
            pl.BlockSpec((1, E_SSM), lambda b, i: (0, 0)),
        ],
        out_specs=[
            pl.BlockSpec((tq, E_SSM), lambda b, i: (b * tiles + i, 0)),
            pl.BlockSpec((1, 1, N_GROUPS * P_STATE), lambda b, i: (b, 0, 0)),
            pl.BlockSpec((1, 1, N_GROUPS * P_STATE), lambda b, i: (b, 0, 0)),
        ],
        out_shape=[jax.ShapeDtypeStruct((BATCH * SEQ, E_SSM), BF16), st, st],
        scratch_shapes=[
            pltpu.VMEM((2, tq, 2 * BLK_ST), F32),
            pltpu.VMEM((2, tq, 2 * BLK_ST), F32),
            pltpu.VMEM((E_SSM // 128, tq, 128), F32),
            pltpu.VMEM((tq, E_SSM), F32),
            pltpu.VMEM((N_BLK, 2, BLK_ST), F32),
        ],
        compiler_params=_cparams(("arbitrary", "arbitrary")),
        name="ssm_prompt",
    )(act, act, bblk, cblk, abar, apow, d_skip)


def _ssm_sample_kernel(u_ref, sz_ref, bblk_ref, cblk_ref, abar_ref, d_ref, s0r_ref, s0i_ref,
                       ys_ref, sre_ref, sim_ref, x_scr, s_scr):
    n_half = BLK_ST // SCAN_W
    rows = DEC_SEQ * DEC_BATCH
    u = u_ref[...]
    ys = []
    for blk in range(N_BLK):
        x = jnp.dot(u[:, blk * BLK_IN:(blk + 1) * BLK_IN], bblk_ref[blk],
                    preferred_element_type=F32)
        for cch in range(2 * n_half):
            x_scr[cch] = x[:, cch * SCAN_W:(cch + 1) * SCAN_W]
        for half in range(n_half):
            lo = half * SCAN_W
            col = blk * BLK_ST + lo
            ar = jnp.broadcast_to(abar_ref[blk, 0:1, lo:lo + SCAN_W], (8, SCAN_W))
            ai = jnp.broadcast_to(abar_ref[blk, 1:2, lo:lo + SCAN_W], (8, SCAN_W))

            def body(rc, carry, half=half, col=col, ar=ar, ai=ai):
                r0 = pl.multiple_of(rc * 8, 8)
                sr = s0r_ref[pl.ds(r0, 8), col:col + SCAN_W]
                si = s0i_ref[pl.ds(r0, 8), col:col + SCAN_W]
                for t in range(DEC_SEQ):
                    row = pl.multiple_of(t * DEC_BATCH + r0, 8)
                    xr = x_scr[half, pl.ds(row, 8), :]
                    xi = x_scr[n_half + half, pl.ds(row, 8), :]
                    sr, si = _cmul_add(ar, ai, sr, si, xr, xi)
                    s_scr[half, pl.ds(row, 8), :] = sr
                    s_scr[n_half + half, pl.ds(row, 8), :] = si
                sre_ref[pl.ds(r0, 8), col:col + SCAN_W] = sr
                sim_ref[pl.ds(r0, 8), col:col + SCAN_W] = si
                return carry

            lax.fori_loop(0, DEC_BATCH // 8, body, 0)

        y = jnp.zeros((rows, BLK_IN), F32)
        for cch in range(2 * n_half):
            y = y + jnp.dot(s_scr[cch].astype(BF16),
                            cblk_ref[blk, cch * SCAN_W:(cch + 1) * SCAN_W, :],
                            preferred_element_type=F32)
        ys.append(y)
    y_all = jnp.concatenate(ys, axis=1)
    yy = (y_all + d_ref[...] * u.astype(F32)) * sz_ref[...].astype(F32)
    ys_ref[...] = jax.nn.gelu(yy).astype(BF16)


def _ssm_sample(act, bblk, cblk, abar, d_skip, s0r, s0i):
    rows = DEC_SEQ * DEC_BATCH
    n_half = BLK_ST // SCAN_W
    st = jax.ShapeDtypeStruct((DEC_BATCH, N_GROUPS * P_STATE), F32)
    full = lambda shape: pl.BlockSpec(shape, lambda i: tuple(0 for _ in shape))
    return pl.pallas_call(
        _ssm_sample_kernel,
        grid=(1,),
        in_specs=[
            pl.BlockSpec((rows, E_SSM), lambda i: (0, 1)),
            pl.BlockSpec((rows, E_SSM), lambda i: (0, 2)),
            full((N_BLK, BLK_IN, 2 * BLK_ST)),
            full((N_BLK, 2 * BLK_ST, BLK_IN)),
            full((N_BLK, 2, BLK_ST)),
            full((1, E_SSM)),
            full((DEC_BATCH, N_GROUPS * P_STATE)),
            full((DEC_BATCH, N_GROUPS * P_STATE)),
        ],
        out_specs=[full((rows, E_SSM)), full((DEC_BATCH, N_GROUPS * P_STATE)),
                   full((DEC_BATCH, N_GROUPS * P_STATE))],
        out_shape=[jax.ShapeDtypeStruct((rows, E_SSM), BF16), st, st],
        scratch_shapes=[
            pltpu.VMEM((2 * n_half, rows, SCAN_W), F32),
            pltpu.VMEM((2 * n_half, rows, SCAN_W), F32),
        ],
        compiler_params=_cparams(("arbitrary",)),
        name="ssm_sample",
    )(act, act, bblk, cblk, abar, d_skip, s0r, s0i)


def _softmax_rows(s):
    m = jnp.max(s, axis=-1, keepdims=True)
    e = jnp.exp(s - m)
    return e / jnp.sum(e, axis=-1, keepdims=True)


def _attn_prompt_kernel(q_ref, az_ref, k_ref, v_ref, o_ref):
    for h in range(N_HEADS):
        cols = slice(h * HEAD_DIM, (h + 1) * HEAD_DIM)
        kh = k_ref[:, cols].astype(BF16)
        vh = v_ref[:, cols].astype(BF16)
        s = lax.dot_general(q_ref[:, cols], kh, (((1,), (1,)), ((), ())),
                            preferred_element_type=F32)
        p = _softmax_rows(s).astype(BF16)
        o = jnp.dot(p, vh, preferred_element_type=F32)
        o_ref[:, cols] = (o * az_ref[:, cols].astype(F32)).astype(BF16)


def _attn_prompt(act, mk2d, mv2d, *, tr):
    tiles = SEQ // tr
    return pl.pallas_call(
        _attn_prompt_kernel,
        grid=(BATCH, tiles),
        in_specs=[
            pl.BlockSpec((tr, E_ATTN), lambda b, i: (b * tiles + i, 3)),
            pl.BlockSpec((tr, E_ATTN), lambda b, i: (b * tiles + i, 4)),
            pl.BlockSpec((MEM_LEN, E_ATTN), lambda b, i: (b, 0)),
            pl.BlockSpec((MEM_LEN, E_ATTN), lambda b, i: (b, 0)),
        ],
        out_specs=pl.BlockSpec((tr, E_ATTN), lambda b, i: (b * tiles + i, 0)),
        out_shape=jax.ShapeDtypeStruct((BATCH * SEQ, E_ATTN), BF16),
        compiler_params=_cparams(("arbitrary", "arbitrary")),
        name="attn_prompt",
    )(act, act, mk2d, mv2d)


def _attn_sample_kernel(q_ref, az_ref, k_ref, v_ref, o_ref, *, bt):
    n_kv = bt * MEM_LEN * N_HEADS
    k = k_ref[...].reshape(n_kv, HEAD_DIM).astype(BF16)
    v = v_ref[...].reshape(n_kv, HEAD_DIM).astype(BF16)
    rows = DEC_SEQ * bt
    q = jnp.concatenate(
        [q_ref[:, :, h * HEAD_DIM:(h + 1) * HEAD_DIM].reshape(rows, HEAD_DIM)
         for h in range(N_HEADS)], axis=0)
    s = lax.dot_general(q, k, (((1,), (1,)), ((), ())), preferred_element_type=F32)
    row = lax.broadcasted_iota(jnp.int32, s.shape, 0)
    col = lax.broadcasted_iota(jnp.int32, s.shape, 1)
    same = ((row // rows == col % N_HEADS)
            & (row % bt == col // (MEM_LEN * N_HEADS)))
    s = jnp.where(same, s, -1e30)
    p = _softmax_rows(s).astype(BF16)
    o = jnp.dot(p, v, preferred_element_type=F32)
    for h in range(N_HEADS):
        cols = slice(h * HEAD_DIM, (h + 1) * HEAD_DIM)
        az = az_ref[:, :, cols].reshape(rows, HEAD_DIM).astype(F32)
        o_ref[:, :, cols] = (o[h * rows:(h + 1) * rows] * az).astype(BF16).reshape(
            DEC_SEQ, bt, HEAD_DIM)


def _attn_sample(act3, k4, v4, *, bt):
    kv_spec = pl.BlockSpec((bt, MEM_LEN, N_HEADS, HEAD_DIM), lambda b: (b, 0, 0, 0))
    return pl.pallas_call(
        functools.partial(_attn_sample_kernel, bt=bt),
        grid=(DEC_BATCH // bt,),
        in_specs=[
            pl.BlockSpec((DEC_SEQ, bt, E_ATTN), lambda b: (0, b, 3)),
            pl.BlockSpec((DEC_SEQ, bt, E_ATTN), lambda b: (0, b, 4)),
            kv_spec, kv_spec,
        ],
        out_specs=pl.BlockSpec((DEC_SEQ, bt, E_ATTN), lambda b: (0, b, 0)),
        out_shape=jax.ShapeDtypeStruct((DEC_SEQ, DEC_BATCH, E_ATTN), BF16),
        compiler_params=_cparams(("arbitrary",)),
        name="attn_sample",
    )(act3, act3, k4, v4)


def _merge_kernel(gated_ref, ys_ref, oz_ref, g0_ref, g1_ref, g2_ref,
                  wco_ref, wga_ref, wgb_ref, wao_ref, o_ref, w_scr):
    @pl.when(pl.program_id(1) == 0)
    def _():
        for k, w_ref in enumerate((wco_ref, wga_ref, wgb_ref, wao_ref)):
            w_scr[k] = w_ref[...].astype(BF16)

    dot = functools.partial(jnp.dot, preferred_element_type=F32)
    conv_out = dot(gated_ref[...], w_scr[0])
    ys = ys_ref[...]
    ssm_out = dot(ys, w_scr[1]) * jax.nn.sigmoid(dot(ys, w_scr[2]))
    attn_out = dot(oz_ref[...], w_scr[3])
    merged = (g0_ref[...].astype(F32) * conv_out + g1_ref[...].astype(F32) * ssm_out
              + g2_ref[...].astype(F32) * attn_out)
    o_ref[...] = merged.astype(BF16)


def _merge(act, ys, oz, wco, wga, wgb, wao, *, tm):
    rows = act.shape[0]
    tn = 1024
    nn = D_MODEL // tn
    wspec = pl.BlockSpec((E_CONV, tn), lambda n, i: (0, n), pipeline_mode=pl.Buffered(1))
    return pl.pallas_call(
        _merge_kernel,
        grid=(nn, rows // tm),
        in_specs=[
            pl.BlockSpec((tm, CB), lambda n, i: (i, 0)),
            pl.BlockSpec((tm, E_SSM), lambda n, i: (i, 0)),
            pl.BlockSpec((tm, E_ATTN), lambda n, i: (i, 0)),
            pl.BlockSpec((tm, tn), lambda n, i: (i, 5 + n)),
            pl.BlockSpec((tm, tn), lambda n, i: (i, 5 + nn + n)),
            pl.BlockSpec((tm, tn), lambda n, i: (i, 5 + 2 * nn + n)),
            wspec, wspec, wspec, wspec,
        ],
        out_specs=pl.BlockSpec((tm, tn), lambda n, i: (i, n)),
        out_shape=jax.ShapeDtypeStruct((rows, D_MODEL), BF16),
        scratch_shapes=[pltpu.VMEM((4, E_CONV, tn), BF16)],
        compiler_params=_cparams(("arbitrary", "arbitrary")),
        name="merge",
    )(act, ys, oz, act, act, act, wco, wga, wgb, wao)


def _out_kernel(x_ref, m_ref, w_ref, g_ref, o_ref):
    y = x_ref[...] + jnp.dot(m_ref[...], w_ref[...], preferred_element_type=F32)
    o_ref[...] = _rms(y, g_ref[...])


def _outproj(x2d, merged, w_bf, g, *, tm):
    rows = x2d.shape[0]
    return pl.pallas_call(
        _out_kernel,
        grid=(rows // tm,),
        in_specs=[
            pl.BlockSpec((tm, D_MODEL), lambda i: (i, 0)),
            pl.BlockSpec((tm, D_MODEL), lambda i: (i, 0)),
            pl.BlockSpec((D_MODEL, D_MODEL), lambda i: (0, 0)),
            pl.BlockSpec((1, D_MODEL), lambda i: (0, 0)),
        ],
        out_specs=pl.BlockSpec((tm, D_MODEL), lambda i: (i, 0)),
        out_shape=jax.ShapeDtypeStruct((rows, D_MODEL), F32),
        compiler_params=_cparams(("arbitrary",)),
        name="outproj",
    )(x2d, merged, w_bf, g)


def kernel(x_prompt, x_sample, mem_prompt, cache_mem_k, cache_mem_v, state_conv, state_ssm_re, state_ssm_im, norm_g, mem_norm_g, w_in, conv_w, w_conv_out, ssm_lambda_re, ssm_lambda_im, ssm_log_dt, ssm_b_re, ssm_b_im, ssm_c_re, ssm_c_im, ssm_d, w_glu_a, w_glu_b, w_mem_k, w_mem_v, w_attn_out, w_out, final_norm_g):
    l = 0
    tq = 512
    bf = lambda w: w.astype(BF16)
    w_in_bf = bf(w_in[l])
    wco, wga, wgb, wao, wo = (w_conv_out[l], w_glu_a[l], w_glu_b[l], w_attn_out[l], bf(w_out[l]))
    g_in = norm_g[l][None, :]
    g_fin = final_norm_g[None, :]
    d_skip = ssm_d[l][None, :]

    n_sq = int(math.log2(tq // SEGS))
    abar, apow, bbr_t, bbi_t = _discretise(ssm_lambda_re[l], ssm_lambda_im[l], ssm_log_dt[l],
                                           ssm_b_re[l], ssm_b_im[l], n_sq)
    bblk, cblk = _block_diag_weights(bbr_t, bbi_t, ssm_c_re[l], ssm_c_im[l])

    xp = x_prompt.reshape(BATCH * SEQ, D_MODEL)
    tm_p = 1024
    act_p, nconv_p = _inproj(xp, g_in, w_in_bf, jnp.zeros((BATCH, 2, E_CONV), F32), conv_w[l],
                             tm=tm_p, shift=1, pad=8, tiles_per_seq=SEQ // tm_p)
    mem2d = mem_prompt.reshape(BATCH * MEM_LEN, D_MODEL)
    g_mem = mem_norm_g[l][None, :]
    mk = _memproj(mem2d, g_mem, w_mem_k[l])
    mv = _memproj(mem2d, g_mem, w_mem_v[l])
    ys_p, sre_p, sim_p = _ssm_prompt(act_p, bblk, cblk, abar, apow, d_skip, tq=tq)
    oz_p = _attn_prompt(act_p, mk, mv, tr=512)
    merged_p = _merge(act_p, ys_p, oz_p, wco, wga, wgb, wao, tm=512)
    y_p = _outproj(xp, merged_p, wo, g_fin, tm=512)

    rows_s = DEC_SEQ * DEC_BATCH
    xs = jnp.transpose(x_sample, (1, 0, 2)).reshape(rows_s, D_MODEL)
    halo_s = jnp.transpose(state_conv[l], (1, 0, 2)).reshape(1, 2 * DEC_BATCH, E_CONV)
    act_s, nconv_s = _inproj(xs, g_in, w_in_bf, halo_s, conv_w[l],
                             tm=rows_s, shift=DEC_BATCH, pad=2 * DEC_BATCH, tiles_per_seq=1)
    s0r = state_ssm_re[l].reshape(DEC_BATCH, N_GROUPS * P_STATE)
    s0i = state_ssm_im[l].reshape(DEC_BATCH, N_GROUPS * P_STATE)
    ys_s, sre_s, sim_s = _ssm_sample(act_s, bblk, cblk, abar, d_skip, s0r, s0i)
    k3 = cache_mem_k[l]
    v3 = cache_mem_v[l]
    oz_s = _attn_sample(act_s.reshape(DEC_SEQ, DEC_BATCH, N_ACT_BLOCKS * CB), k3, v3, bt=8)
    merged_s = _merge(act_s, ys_s, oz_s.reshape(rows_s, E_ATTN), wco, wga, wgb, wao, tm=512)
    y_s = _outproj(xs, merged_s, wo, g_fin, tm=512)

    y_prompt = y_p.reshape(BATCH, SEQ, D_MODEL)
    y_sample = jnp.transpose(y_s.reshape(DEC_SEQ, DEC_BATCH, D_MODEL), (1, 0, 2))
    st_shape_p = (1, BATCH, N_GROUPS, P_STATE)
    st_shape_s = (1, DEC_BATCH, N_GROUPS, P_STATE)
    new_conv_s = jnp.transpose(nconv_s.reshape(2, DEC_BATCH, E_CONV), (1, 0, 2))
    return (y_prompt, y_sample,
            mk.reshape(1, BATCH, MEM_LEN, N_HEADS, HEAD_DIM),
            mv.reshape(1, BATCH, MEM_LEN, N_HEADS, HEAD_DIM),
            nconv_p[None],
            sre_p.reshape(st_shape_p), sim_p.reshape(st_shape_p),
            new_conv_s[None],
            sre_s.reshape(st_shape_s), sim_s.reshape(st_shape_s))
```

```python
import functools
import math

import jax
import jax.numpy as jnp
from jax import lax
from jax.experimental import pallas as pl
from jax.experimental.pallas import tpu as pltpu

D_MODEL = 2048
BATCH = 4
SEQ = 2048
DEC_BATCH = 128
DEC_SEQ = 4
E_CONV = 1024
CONV_W = 3
E_SSM = 1024
GROUP = 16
N_GROUPS = 64
P_STATE = 64
N_HEADS = 4
HEAD_DIM = 256
E_ATTN = 1024
MEM_LEN = 256
N_IN = 14336
EPS = 1e-6

F32 = jnp.float32
BF16 = jnp.bfloat16

CB = 1024
N_ACT_BLOCKS = 11
GROUPS_PER_BLK = 16
N_BLK = N_GROUPS // GROUPS_PER_BLK
BLK_IN = GROUPS_PER_BLK * GROUP
BLK_ST = GROUPS_PER_BLK * P_STATE
SCAN_W = 512
SEGS = 8
VMEM_LIMIT = 56 * 1024 * 1024


def _cparams(sem):
    return pltpu.CompilerParams(dimension_semantics=sem, vmem_limit_bytes=VMEM_LIMIT)


def _rms(x, g):
    ms = jnp.mean(x * x, axis=-1, keepdims=True)
    return x * lax.rsqrt(ms + EPS) * g


def _inproj_kernel(x_ref, g_ref, w_ref, halo_ref, cw_ref, act_ref, nconv_ref, *rest,
                   tm, shift, pad, tiles_per_seq, emit_w):
    i = pl.program_id(0)
    j = pl.program_id(1)
    if emit_w:
        wbf_ref, h_scr, cb_scr, vext_scr = rest
        wbf_ref[...] = w_ref[...].astype(BF16)
        w_ref = wbf_ref
    else:
        h_scr, cb_scr, vext_scr = rest

    @pl.when(j == 0)
    def _():
        h_scr[...] = _rms(x_ref[...], g_ref[...]).astype(BF16)

    def proj():
        return jnp.dot(h_scr[...], w_ref[...], preferred_element_type=F32)

    @pl.when(j == 0)
    def _():
        cb_scr[...] = proj()

    @pl.when(j == 1)
    def _():
        vext_scr[pl.ds(pad, tm), :] = proj()

    @pl.when(j == 2)
    def _():
        vext_scr[pl.ds(pad, tm), :] = vext_scr[pl.ds(pad, tm), :] * proj()

        @pl.when(i % tiles_per_seq == 0)
        def _():
            vext_scr[pl.ds(pad - 2 * shift, 2 * shift), :] = halo_ref[0]

    @pl.when(j == 3)
    def _():
        conv = (cw_ref[0:1, :] * vext_scr[pl.ds(pad - 2 * shift, tm), :]
                + cw_ref[1:2, :] * vext_scr[pl.ds(pad - shift, tm), :]
                + cw_ref[2:3, :] * vext_scr[pl.ds(pad, tm), :])
        act_ref[...] = (cb_scr[...] * conv * jax.nn.silu(proj())).astype(BF16)
        tail = vext_scr[pl.ds(pad + tm - 2 * shift, 2 * shift), :]
        nconv_ref[0] = tail
        vext_scr[pl.ds(pad - 2 * shift, 2 * shift), :] = tail

    @pl.when(j == 4)
    def _():
        act_ref[...] = proj().astype(BF16)

    @pl.when(j == 6)
    def _():
        act_ref[...] = (proj() * (HEAD_DIM ** -0.5)).astype(BF16)

    @pl.when((j == 5) | (j == 7))
    def _():
        act_ref[...] = jax.nn.silu(proj()).astype(BF16)

    @pl.when(j >= 8)
    def _():
        act_ref[...] = jax.nn.sigmoid(proj()).astype(BF16)


def _inproj(x2d, g, w, halo, conv_w, *, tm, shift, pad, tiles_per_seq):
    rows = x2d.shape[0]
    n_seq = rows // (tm * tiles_per_seq)
    emit_w = w.dtype == F32
    assert not emit_w or rows == tm
    kern = functools.partial(_inproj_kernel, tm=tm, shift=shift, pad=pad,
                             tiles_per_seq=tiles_per_seq, emit_w=emit_w)
    w_out_spec = [pl.BlockSpec((D_MODEL, CB), lambda i, j: (0, j))] if emit_w else []
    w_out_shape = [jax.ShapeDtypeStruct((D_MODEL, N_IN), BF16)] if emit_w else []
    return pl.pallas_call(
        kern,
        grid=(rows // tm, N_IN // CB),
        in_specs=[
            pl.BlockSpec((tm, D_MODEL), lambda i, j: (i, 0)),
            pl.BlockSpec((1, D_MODEL), lambda i, j: (0, 0)),
            pl.BlockSpec((D_MODEL, CB), lambda i, j: (0, j)),
            pl.BlockSpec((1, 2 * shift, E_CONV), lambda i, j: (i // tiles_per_seq, 0, 0)),
            pl.BlockSpec((CONV_W, E_CONV), lambda i, j: (0, 0)),
        ],
        out_specs=[
            pl.BlockSpec((tm, CB), lambda i, j: (i, jnp.maximum(j - 3, 0))),
            pl.BlockSpec((1, 2 * shift, E_CONV), lambda i, j: (i // tiles_per_seq, 0, 0)),
        ] + w_out_spec,
        out_shape=[
            jax.ShapeDtypeStruct((rows, N_ACT_BLOCKS * CB), BF16),
            jax.ShapeDtypeStruct((n_seq, 2 * shift, E_CONV), F32),
        ] + w_out_shape,
        scratch_shapes=[
            pltpu.VMEM((tm, D_MODEL), BF16),
            pltpu.VMEM((tm, E_CONV), F32),
            pltpu.VMEM((pad + tm, E_CONV), F32),
        ],
        compiler_params=_cparams(("arbitrary", "arbitrary")),
        name="inproj",
    )(x2d, g, w, halo, conv_w)


def _memproj_kernel(x_ref, g_ref, w_ref, o_ref):
    h = _rms(x_ref[...], g_ref[...]).astype(BF16)
    o_ref[...] = jnp.dot(h, w_ref[...].astype(BF16), preferred_element_type=F32)


def _memproj(mem2d, g, w_bf):
    rows = mem2d.shape[0]
    tm = 512
    return pl.pallas_call(
        _memproj_kernel,
        grid=(rows // tm,),
        in_specs=[
            pl.BlockSpec((tm, D_MODEL), lambda i: (i, 0)),
            pl.BlockSpec((1, D_MODEL), lambda i: (0, 0)),
            pl.BlockSpec((D_MODEL, E_ATTN), lambda i: (0, 0)),
        ],
        out_specs=pl.BlockSpec((tm, E_ATTN), lambda i: (i, 0)),
        out_shape=jax.ShapeDtypeStruct((rows, E_ATTN), F32),
        compiler_params=_cparams(("arbitrary",)),
        name="memproj",
    )(mem2d, g, w_bf)


def _disc_kernel(lr_ref, li_ref, ldt_ref, bre_ref, bim_ref,
                 ar_ref, ai_ref, pr_ref, pi_ref, bbr_ref, bbi_ref, *, n_sq):
    lr = lr_ref[...]
    li = li_ref[...]
    dt = jnp.exp(ldt_ref[...])
    mag = jnp.exp(lr * dt)
    ang = li * dt
    ar = mag * jnp.cos(ang)
    ai = mag * jnp.sin(ang)
    den = lr * lr + li * li
    fr = ((ar - 1.0) * lr + ai * li) / den
    fi = (ai * lr - (ar - 1.0) * li) / den
    bre = bre_ref[...]
    bim = bim_ref[...]
    bbr_ref[...] = fr * bre - fi * bim
    bbi_ref[...] = fr * bim + fi * bre
    ar_ref[...] = ar
    ai_ref[...] = ai
    pr, pi = ar, ai
    for _ in range(n_sq):
        pr, pi = pr * pr - pi * pi, 2.0 * pr * pi
    pr_ref[...] = pr
    pi_ref[...] = pi


def _discretise(lam_re, lam_im, log_dt, b_re, b_im, n_sq):
    rows = N_GROUPS * GROUP
    rep = lambda a: jnp.repeat(a, GROUP, axis=0)
    lr = rep(lam_re)
    li = rep(lam_im)
    ldt = rep(jnp.broadcast_to(log_dt[:, None], (N_GROUPS, P_STATE)))
    bre_t = jnp.transpose(b_re, (0, 2, 1)).reshape(rows, P_STATE)
    bim_t = jnp.transpose(b_im, (0, 2, 1)).reshape(rows, P_STATE)
    sds = jax.ShapeDtypeStruct((rows, P_STATE), F32)
    outs = pl.pallas_call(
        functools.partial(_disc_kernel, n_sq=n_sq),
        out_shape=[sds] * 6,
        name="s5_disc",
    )(lr, li, ldt, bre_t, bim_t)
    ar, ai, pr, pi, bbr_t, bbi_t = outs
    pick = lambda a: a.reshape(N_GROUPS, GROUP, P_STATE)[:, 0, :].reshape(N_BLK, 1, BLK_ST)
    abar = jnp.concatenate([pick(ar), pick(ai)], axis=1)
    apow = jnp.concatenate([pick(pr), pick(pi)], axis=1)
    return abar, apow, bbr_t, bbi_t


def _block_diag_weights(bbr_t, bbi_t, c_re, c_im):
    st = jnp.arange(BLK_ST)
    spread = (jnp.arange(P_STATE)[:, None] == st[None, :] % P_STATE).astype(F32)
    same_group = (jnp.arange(BLK_IN)[:, None] // GROUP == st[None, :] // P_STATE).astype(F32)
    hi = lax.Precision.HIGHEST

    def b_side(a):
        a = a.reshape(N_BLK, BLK_IN, P_STATE)
        return jnp.einsum('bkp,pn->bkn', a, spread, precision=hi) * same_group

    def c_side(a):
        a = a.reshape(N_BLK, BLK_IN, P_STATE)
        return jnp.einsum('pn,bkp->bnk', spread, a, precision=hi) * same_group.T

    bblk = jnp.concatenate([b_side(bbr_t), b_side(bbi_t)], axis=2).astype(BF16)
    cblk = jnp.concatenate([c_side(c_re), c_side(-c_im)], axis=1).astype(BF16)
    return bblk, cblk


def _cmul_add(ar, ai, sr, si, xr, xi):
    return ar * sr - ai * si + xr, ar * si + ai * sr + xi


def _ssm_prompt_kernel(u_ref, sz_ref, bblk_ref, cblk_ref, abar_ref, apow_ref, d_ref,
                       ys_ref, sre_ref, sim_ref,
                       x_scr, s_scr, y_scr, ynat_scr, carry_scr, *, tq):
    i = pl.program_id(1)
    seg_len = tq // SEGS
    n_half = BLK_ST // SCAN_W

    @pl.when(i == 0)
    def _():
        carry_scr[...] = jnp.zeros_like(carry_scr)

    u = u_ref[...]
    r = lax.broadcasted_iota(jnp.int32, (tq, tq), 0)
    c = lax.broadcasted_iota(jnp.int32, (tq, tq), 1)
    perm = jnp.where(c == (r % SEGS) * seg_len + r // SEGS, 1.0, 0.0).astype(BF16)
    u_perm = jnp.dot(perm, u, preferred_element_type=F32).astype(BF16)
    row_id = lax.broadcasted_iota(jnp.int32, (SEGS, SCAN_W), 0)

    for blk in range(N_BLK):
        par = blk % 2
        x_scr[par] = jnp.dot(u_perm[:, blk * BLK_IN:(blk + 1) * BLK_IN], bblk_ref[blk],
                             preferred_element_type=F32)
        for half in range(n_half):
            lo = half * SCAN_W
            re_cols = slice(lo, lo + SCAN_W)
            im_cols = slice(BLK_ST + lo, BLK_ST + lo + SCAN_W)
            ar = jnp.broadcast_to(abar_ref[blk, 0:1, lo:lo + SCAN_W], (SEGS, SCAN_W))
            ai = jnp.broadcast_to(abar_ref[blk, 1:2, lo:lo + SCAN_W], (SEGS, SCAN_W))

            def local_end(j, carry, par=par, re_cols=re_cols, im_cols=im_cols, ar=ar, ai=ai):
                row = pl.multiple_of(j * SEGS, SEGS)
                xr = x_scr[par, pl.ds(row, SEGS), re_cols]
                xi = x_scr[par, pl.ds(row, SEGS), im_cols]
                return _cmul_add(ar, ai, carry[0], carry[1], xr, xi)

            zero = jnp.zeros((SEGS, SCAN_W), F32)
            er, ei = lax.fori_loop(0, seg_len, local_end, (zero, zero), unroll=True)

            pr = apow_ref[blk, 0:1, lo:lo + SCAN_W]
            pi = apow_ref[blk, 1:2, lo:lo + SCAN_W]
            prev_r = carry_scr[blk, 0:1, lo:lo + SCAN_W]
            prev_i = carry_scr[blk, 1:2, lo:lo + SCAN_W]
            init_r = zero
            init_i = zero
            for k in range(SEGS):
                init_r = jnp.where(row_id == k, prev_r, init_r)
                init_i = jnp.where(row_id == k, prev_i, init_i)
                prev_r, prev_i = _cmul_add(pr, pi, prev_r, prev_i,
                                           er[k:k + 1, :], ei[k:k + 1, :])
            carry_scr[blk, 0:1, lo:lo + SCAN_W] = prev_r
            carry_scr[blk, 1:2, lo:lo + SCAN_W] = prev_i

            def full_scan(j, carry, par=par, re_cols=re_cols, im_cols=im_cols, ar=ar, ai=ai):
                row = pl.multiple_of(j * SEGS, SEGS)
                xr = x_scr[par, pl.ds(row, SEGS), re_cols]
                xi = x_scr[par, pl.ds(row, SEGS), im_cols]
                sr, si = _cmul_add(ar, ai, carry[0], carry[1], xr, xi)
                s_scr[par, pl.ds(row, SEGS), re_cols] = sr
                s_scr[par, pl.ds(row, SEGS), im_cols] = si
                return sr, si

            lax.fori_loop(0, seg_len, full_scan, (init_r, init_i), unroll=True)

        y = jnp.dot(s_scr[par].astype(BF16), cblk_ref[blk], preferred_element_type=F32)
        for sl in range(BLK_IN // 128):
            y_scr[blk * (BLK_IN // 128) + sl] = y[:, sl * 128:(sl + 1) * 128]

    for slab in range(E_SSM // 128):
        for k in range(SEGS):
            ynat_scr[k * seg_len:(k + 1) * seg_len, slab * 128:(slab + 1) * 128] = (
                y_scr[slab, pl.ds(k, seg_len, stride=SEGS), :])

    yy = (ynat_scr[...] + d_ref[...] * u.astype(F32)) * sz_ref[...].astype(F32)
    ys_ref[...] = jax.nn.gelu(yy).astype(BF16)
    for blk in range(N_BLK):
        sre_ref[0, :, blk * BLK_ST:(blk + 1) * BLK_ST] = carry_scr[blk, 0:1, :]
        sim_ref[0, :, blk * BLK_ST:(blk + 1) * BLK_ST] = carry_scr[blk, 1:2, :]


def _ssm_prompt(act, bblk, cblk, abar, apow, d_skip, *, tq):
    tiles = SEQ // tq
    n_half = BLK_ST // SCAN_W
    st = jax.ShapeDtypeStruct((BATCH, 1, N_GROUPS * P_STATE), F32)
    const3 = lambda b, i: (0, 0, 0)
    return pl.pallas_call(
        functools.partial(_ssm_prompt_kernel, tq=tq),
        grid=(BATCH, tiles),
        in_specs=[
            pl.BlockSpec((tq, E_SSM), lambda b, i: (b * tiles + i, 1)),
            pl.BlockSpec((tq, E_SSM), lambda b, i: (b * tiles + i, 2)),
            pl.BlockSpec((N_BLK, BLK_IN, 2 * BLK_ST), const3),
            pl.BlockSpec((N_BLK, 2 * BLK_ST, BLK_IN), const3),
            pl.BlockSpec((N_BLK, 2, BLK_ST), const3),
            pl.BlockSpec((N_BLK, 2, BLK_ST), const3),
            pl.BlockSpec((1, E_SSM), lambda b, i: (0, 0)),
        ],
        out_specs=[
            pl.BlockSpec((tq, E_SSM), lambda b, i: (b * tiles + i, 0)),
            pl.BlockSpec((1, 1, N_GROUPS * P_STATE), lambda b, i: (b, 0, 0)),
            pl.BlockSpec((1, 1, N_GROUPS * P_STATE), lambda b, i: (b, 0, 0)),
        ],
        out_shape=[jax.ShapeDtypeStruct((BATCH * SEQ, E_SSM), BF16), st, st],
        scratch_shapes=[
            pltpu.VMEM((2, tq, 2 * BLK_ST), F32),
            pltpu.VMEM((2, tq, 2 * BLK_ST), F32),
            pltpu.VMEM((E_SSM // 128, tq, 128), F32),
            pltpu.VMEM((tq, E_SSM), F32),
            pltpu.VMEM((N_BLK, 2, BLK_ST), F32),
        ],
        compiler_params=_cparams(("arbitrary", "arbitrary")),
        name="ssm_prompt",
    )(act, act, bblk, cblk, abar, apow, d_skip)


def _ssm_sample_kernel(u_ref, sz_ref, bblk_ref, cblk_ref, abar_ref, d_ref, s0r_ref, s0i_ref,
                       ys_ref, sre_ref, sim_ref, x_scr, s_scr):
    n_half = BLK_ST // SCAN_W
    rows = DEC_SEQ * DEC_BATCH
    u = u_ref[...]
    ys = []
    for blk in range(N_BLK):
        x = jnp.dot(u[:, blk * BLK_IN:(blk + 1) * BLK_IN], bblk_ref[blk],
                    preferred_element_type=F32)
        for cch in range(2 * n_half):
            x_scr[cch] = x[:, cch * SCAN_W:(cch + 1) * SCAN_W]
        for half in range(n_half):
            lo = half * SCAN_W
            col = blk * BLK_ST + lo
            ar = jnp.broadcast_to(abar_ref[blk, 0:1, lo:lo + SCAN_W], (8, SCAN_W))
            ai = jnp.broadcast_to(abar_ref[blk, 1:2, lo:lo + SCAN_W], (8, SCAN_W))

            def body(rc, carry, half=half, col=col, ar=ar, ai=ai):
                r0 = pl.multiple_of(rc * 8, 8)
                sr = s0r_ref[pl.ds(r0, 8), col:col + SCAN_W]
                si = s0i_ref[pl.ds(r0, 8), col:col + SCAN_W]
                for t in range(DEC_SEQ):
                    row = pl.multiple_of(t * DEC_BATCH + r0, 8)
                    xr = x_scr[half, pl.ds(row, 8), :]
                    xi = x_scr[n_half + half, pl.ds(row, 8), :]
                    sr, si = _cmul_add(ar, ai, sr, si, xr, xi)
                    s_scr[half, pl.ds(row, 8), :] = sr
                    s_scr[n_half + half, pl.ds(row, 8), :] = si
                sre_ref[pl.ds(r0, 8), col:col + SCAN_W] = sr
                sim_ref[pl.ds(r0, 8), col:col + SCAN_W] = si
                return carry

            lax.fori_loop(0, DEC_BATCH // 8, body, 0)

        y = jnp.zeros((rows, BLK_IN), F32)
        for cch in range(2 * n_half):
            y = y + jnp.dot(s_scr[cch].astype(BF16),
                            cblk_ref[blk, cch * SCAN_W:(cch + 1) * SCAN_W, :],
                            preferred_element_type=F32)
        ys.append(y)
    y_all = jnp.concatenate(ys, axis=1)
    yy = (y_all + d_ref[...] * u.astype(F32)) * sz_ref[...].astype(F32)
    ys_ref[...] = jax.nn.gelu(yy).astype(BF16)


def _ssm_sample(act, bblk, cblk, abar, d_skip, s0r, s0i):
    rows = DEC_SEQ * DEC_BATCH
    n_half = BLK_ST // SCAN_W
    st = jax.ShapeDtypeStruct((DEC_BATCH, N_GROUPS * P_STATE), F32)
    full = lambda shape: pl.BlockSpec(shape, lambda i: tuple(0 for _ in shape))
    return pl.pallas_call(
        _ssm_sample_kernel,
        grid=(1,),
        in_specs=[
            pl.BlockSpec((rows, E_SSM), lambda i: (0, 1)),
            pl.BlockSpec((rows, E_SSM), lambda i: (0, 2)),
            full((N_BLK, BLK_IN, 2 * BLK_ST)),
            full((N_BLK, 2 * BLK_ST, BLK_IN)),
            full((N_BLK, 2, BLK_ST)),
            full((1, E_SSM)),
            full((DEC_BATCH, N_GROUPS * P_STATE)),
            full((DEC_BATCH, N_GROUPS * P_STATE)),
        ],
        out_specs=[full((rows, E_SSM)), full((DEC_BATCH, N_GROUPS * P_STATE)),
                   full((DEC_BATCH, N_GROUPS * P_STATE))],
        out_shape=[jax.ShapeDtypeStruct((rows, E_SSM), BF16), st, st],
        scratch_shapes=[
            pltpu.VMEM((2 * n_half, rows, SCAN_W), F32),
            pltpu.VMEM((2 * n_half, rows, SCAN_W), F32),
        ],
        compiler_params=_cparams(("arbitrary",)),
        name="ssm_sample",
    )(act, act, bblk, cblk, abar, d_skip, s0r, s0i)


def _softmax_rows(s):
    m = jnp.max(s, axis=-1, keepdims=True)
    e = jnp.exp(s - m)
    return e / jnp.sum(e, axis=-1, keepdims=True)


def _attn_prompt_kernel(q_ref, az_ref, k_ref, v_ref, o_ref):
    for h in range(N_HEADS):
        cols = slice(h * HEAD_DIM, (h + 1) * HEAD_DIM)
        kh = k_ref[:, cols].astype(BF16)
        vh = v_ref[:, cols].astype(BF16)
        s = lax.dot_general(q_ref[:, cols], kh, (((1,), (1,)), ((), ())),
                            preferred_element_type=F32)
        p = _softmax_rows(s).astype(BF16)
        o = jnp.dot(p, vh, preferred_element_type=F32)
        o_ref[:, cols] = (o * az_ref[:, cols].astype(F32)).astype(BF16)


def _attn_prompt(act, mk2d, mv2d, *, tr):
    tiles = SEQ // tr
    return pl.pallas_call(
        _attn_prompt_kernel,
        grid=(BATCH, tiles),
        in_specs=[
            pl.BlockSpec((tr, E_ATTN), lambda b, i: (b * tiles + i, 3)),
            pl.BlockSpec((tr, E_ATTN), lambda b, i: (b * tiles + i, 4)),
            pl.BlockSpec((MEM_LEN, E_ATTN), lambda b, i: (b, 0)),
            pl.BlockSpec((MEM_LEN, E_ATTN), lambda b, i: (b, 0)),
        ],
        out_specs=pl.BlockSpec((tr, E_ATTN), lambda b, i: (b * tiles + i, 0)),
        out_shape=jax.ShapeDtypeStruct((BATCH * SEQ, E_ATTN), BF16),
        compiler_params=_cparams(("arbitrary", "arbitrary")),
        name="attn_prompt",
    )(act, act, mk2d, mv2d)


def _attn_sample_kernel(q_ref, az_ref, k_ref, v_ref, o_ref, *, bt):
    n_kv = bt * MEM_LEN * N_HEADS
    k = k_ref[...].reshape(n_kv, HEAD_DIM).astype(BF16)
    v = v_ref[...].reshape(n_kv, HEAD_DIM).astype(BF16)
    rows = DEC_SEQ * bt
    q = jnp.concatenate(
        [q_ref[:, :, h * HEAD_DIM:(h + 1) * HEAD_DIM].reshape(rows, HEAD_DIM)
         for h in range(N_HEADS)], axis=0)
    s = lax.dot_general(q, k, (((1,), (1,)), ((), ())), preferred_element_type=F32)
    row = lax.broadcasted_iota(jnp.int32, s.shape, 0)
    col = lax.broadcasted_iota(jnp.int32, s.shape, 1)
    same = ((row // rows == col % N_HEADS)
            & (row % bt == col // (MEM_LEN * N_HEADS)))
    s = jnp.where(same, s, -1e30)
    p = _softmax_rows(s).astype(BF16)
    o = jnp.dot(p, v, preferred_element_type=F32)
    for h in range(N_HEADS):
        cols = slice(h * HEAD_DIM, (h + 1) * HEAD_DIM)
        az = az_ref[:, :, cols].reshape(rows, HEAD_DIM).astype(F32)
        o_ref[:, :, cols] = (o[h * rows:(h + 1) * rows] * az).astype(BF16).reshape(
            DEC_SEQ, bt, HEAD_DIM)


def _attn_sample(act3, k4, v4, *, bt):
    kv_spec = pl.BlockSpec((bt, MEM_LEN, N_HEADS, HEAD_DIM), lambda b: (b, 0, 0, 0))
    return pl.pallas_call(
        functools.partial(_attn_sample_kernel, bt=bt),
        grid=(DEC_BATCH // bt,),
        in_specs=[
            pl.BlockSpec((DEC_SEQ, bt, E_ATTN), lambda b: (0, b, 3)),
            pl.BlockSpec((DEC_SEQ, bt, E_ATTN), lambda b: (0, b, 4)),
            kv_spec, kv_spec,
        ],
        out_specs=pl.BlockSpec((DEC_SEQ, bt, E_ATTN), lambda b: (0, b, 0)),
        out_shape=jax.ShapeDtypeStruct((DEC_SEQ, DEC_BATCH, E_ATTN), BF16),
        compiler_params=_cparams(("arbitrary",)),
        name="attn_sample",
    )(act3, act3, k4, v4)


def _merge_kernel(gated_ref, ys_ref, oz_ref, g0_ref, g1_ref, g2_ref,
                  wco_ref, wga_ref, wgb_ref, wao_ref, o_ref, w_scr):
    @pl.when(pl.program_id(1) == 0)
    def _():
        for k, w_ref in enumerate((wco_ref, wga_ref, wgb_ref, wao_ref)):
            w_scr[k] = w_ref[...].astype(BF16)

    dot = functools.partial(jnp.dot, preferred_element_type=F32)
    conv_out = dot(gated_ref[...], w_scr[0])
    ys = ys_ref[...]
    ssm_out = dot(ys, w_scr[1]) * jax.nn.sigmoid(dot(ys, w_scr[2]))
    attn_out = dot(oz_ref[...], w_scr[3])
    merged = (g0_ref[...].astype(F32) * conv_out + g1_ref[...].astype(F32) * ssm_out
              + g2_ref[...].astype(F32) * attn_out)
    o_ref[...] = merged.astype(BF16)


def _merge(act, ys, oz, wco, wga, wgb, wao, *, tm):
    rows = act.shape[0]
    tn = 1024
    nn = D_MODEL // tn
    wspec = pl.BlockSpec((E_CONV, tn), lambda n, i: (0, n), pipeline_mode=pl.Buffered(1))
    return pl.pallas_call(
        _merge_kernel,
        grid=(nn, rows // tm),
        in_specs=[
            pl.BlockSpec((tm, CB), lambda n, i: (i, 0)),
            pl.BlockSpec((tm, E_SSM), lambda n, i: (i, 0)),
            pl.BlockSpec((tm, E_ATTN), lambda n, i: (i, 0)),
            pl.BlockSpec((tm, tn), lambda n, i: (i, 5 + n)),
            pl.BlockSpec((tm, tn), lambda n, i: (i, 5 + nn + n)),
            pl.BlockSpec((tm, tn), lambda n, i: (i, 5 + 2 * nn + n)),
            wspec, wspec, wspec, wspec,
        ],
        out_specs=pl.BlockSpec((tm, tn), lambda n, i: (i, n)),
        out_shape=jax.ShapeDtypeStruct((rows, D_MODEL), BF16),
        scratch_shapes=[pltpu.VMEM((4, E_CONV, tn), BF16)],
        compiler_params=_cparams(("arbitrary", "arbitrary")),
        name="merge",
    )(act, ys, oz, act, act, act, wco, wga, wgb, wao)


def _out_kernel(x_ref, m_ref, w_ref, g_ref, o_ref):
    y = x_ref[...] + jnp.dot(m_ref[...], w_ref[...], preferred_element_type=F32)
    o_ref[...] = _rms(y, g_ref[...])


def _outproj(x2d, merged, w_bf, g, *, tm):
    rows = x2d.shape[0]
    return pl.pallas_call(
        _out_kernel,
        grid=(rows // tm,),
        in_specs=[
            pl.BlockSpec((tm, D_MODEL), lambda i: (i, 0)),
            pl.BlockSpec((tm, D_MODEL), lambda i: (i, 0)),
            pl.BlockSpec((D_MODEL, D_MODEL), lambda i: (0, 0)),
            pl.BlockSpec((1, D_MODEL), lambda i: (0, 0)),
        ],
        out_specs=pl.BlockSpec((tm, D_MODEL), lambda i: (i, 0)),
        out_shape=jax.ShapeDtypeStruct((rows, D_MODEL), F32),
        compiler_params=_cparams(("arbitrary",)),
        name="outproj",
    )(x2d, merged, w_bf, g)


def kernel(x_prompt, x_sample, mem_prompt, cache_mem_k, cache_mem_v, state_conv, state_ssm_re, state_ssm_im, norm_g, mem_norm_g, w_in, conv_w, w_conv_out, ssm_lambda_re, ssm_lambda_im, ssm_log_dt, ssm_b_re, ssm_b_im, ssm_c_re, ssm_c_im, ssm_d, w_glu_a, w_glu_b, w_mem_k, w_mem_v, w_attn_out, w_out, final_norm_g):
    l = 0
    tq = 512
    bf = lambda w: w.astype(BF16)
    wco, wga, wgb, wao, wo = (w_conv_out[l], w_glu_a[l], w_glu_b[l], w_attn_out[l], bf(w_out[l]))
    g_in = norm_g[l][None, :]
    g_fin = final_norm_g[None, :]
    d_skip = ssm_d[l][None, :]

    n_sq = int(math.log2(tq // SEGS))
    abar, apow, bbr_t, bbi_t = _discretise(ssm_lambda_re[l], ssm_lambda_im[l], ssm_log_dt[l],
                                           ssm_b_re[l], ssm_b_im[l], n_sq)
    bblk, cblk = _block_diag_weights(bbr_t, bbi_t, ssm_c_re[l], ssm_c_im[l])

    rows_s = DEC_SEQ * DEC_BATCH
    xs = jnp.transpose(x_sample, (1, 0, 2)).reshape(rows_s, D_MODEL)
    halo_s = jnp.transpose(state_conv[l], (1, 0, 2)).reshape(1, 2 * DEC_BATCH, E_CONV)
    act_s, nconv_s, w_in_bf = _inproj(xs, g_in, w_in[l], halo_s, conv_w[l], tm=rows_s,
                                      shift=DEC_BATCH, pad=2 * DEC_BATCH, tiles_per_seq=1)

    xp = x_prompt.reshape(BATCH * SEQ, D_MODEL)
    tm_p = 1024
    act_p, nconv_p = _inproj(xp, g_in, w_in_bf, jnp.zeros((BATCH, 2, E_CONV), F32), conv_w[l],
                             tm=tm_p, shift=1, pad=8, tiles_per_seq=SEQ // tm_p)
    mem2d = mem_prompt.reshape(BATCH * MEM_LEN, D_MODEL)
    g_mem = mem_norm_g[l][None, :]
    mk = _memproj(mem2d, g_mem, w_mem_k[l])
    mv = _memproj(mem2d, g_mem, w_mem_v[l])
    ys_p, sre_p, sim_p = _ssm_prompt(act_p, bblk, cblk, abar, apow, d_skip, tq=tq)
    oz_p = _attn_prompt(act_p, mk, mv, tr=512)
    merged_p = _merge(act_p, ys_p, oz_p, wco, wga, wgb, wao, tm=512)
    y_p = _outproj(xp, merged_p, wo, g_fin, tm=512)

    s0r = state_ssm_re[l].reshape(DEC_BATCH, N_GROUPS * P_STATE)
    s0i = state_ssm_im[l].reshape(DEC_BATCH, N_GROUPS * P_STATE)
    ys_s, sre_s, sim_s = _ssm_sample(act_s, bblk, cblk, abar, d_skip, s0r, s0i)
    k3 = cache_mem_k[l]
    v3 = cache_mem_v[l]
    oz_s = _attn_sample(act_s.reshape(DEC_SEQ, DEC_BATCH, N_ACT_BLOCKS * CB), k3, v3, bt=8)
    merged_s = _merge(act_s, ys_s, oz_s.reshape(rows_s, E_ATTN), wco, wga, wgb, wao, tm=512)
    y_s = _outproj(xs, merged_s, wo, g_fin, tm=512)

    y_prompt = y_p.reshape(BATCH, SEQ, D_MODEL)
    y_sample = jnp.transpose(y_s.reshape(DEC_SEQ, DEC_BATCH, D_MODEL), (1, 0, 2))
    st_shape_p = (1, BATCH, N_GROUPS, P_STATE)
    st_shape_s = (1, DEC_BATCH, N_GROUPS, P_STATE)
    new_conv_s = jnp.transpose(nconv_s.reshape(2, DEC_BATCH, E_CONV), (1, 0, 2))
    return (y_prompt, y_sample,
            mk.reshape(1, BATCH, MEM_LEN, N_HEADS, HEAD_DIM),
            mv.reshape(1, BATCH, MEM_LEN, N_HEADS, HEAD_DIM),
            nconv_p[None],
            sre_p.reshape(st_shape_p), sim_p.reshape(st_shape_p),
            new_conv_s[None],
            sre_s.reshape(st_shape_s), sim_s.reshape(st_shape_s))
```

```python
import functools
import math

import jax
import jax.numpy as jnp
from jax import lax
from jax.experimental import pallas as pl
from jax.experimental.pallas import tpu as pltpu

D_MODEL = 2048
BATCH = 4
SEQ = 2048
DEC_BATCH = 128
DEC_SEQ = 4
E_CONV = 1024
CONV_W = 3
E_SSM = 1024
GROUP = 16
N_GROUPS = 64
P_STATE = 64
N_HEADS = 4
HEAD_DIM = 256
E_ATTN = 1024
MEM_LEN = 256
N_IN = 14336
EPS = 1e-6

F32 = jnp.float32
BF16 = jnp.bfloat16

CB = 1024
N_ACT_BLOCKS = 11
GROUPS_PER_BLK = 16
N_BLK = N_GROUPS // GROUPS_PER_BLK
BLK_IN = GROUPS_PER_BLK * GROUP
BLK_ST = GROUPS_PER_BLK * P_STATE
SCAN_W = 512
SEGS = 8
VMEM_LIMIT = 56 * 1024 * 1024


def _cparams(sem):
    return pltpu.CompilerParams(dimension_semantics=sem, vmem_limit_bytes=VMEM_LIMIT)


def _sigmoid(x):
    return 0.5 * jnp.tanh(0.5 * x) + 0.5


def _silu(x):
    return x * _sigmoid(x)


def _rms(x, g):
    ms = jnp.mean(x * x, axis=-1, keepdims=True)
    return x * lax.rsqrt(ms + EPS) * g


def _inproj_kernel(x_ref, g_ref, w_ref, halo_ref, cw_ref, act_ref, nconv_ref, *rest,
                   tm, shift, pad, tiles_per_seq, emit_w):
    i = pl.program_id(0)
    j = pl.program_id(1)
    if emit_w:
        wbf_ref, h_scr, cb_scr, vext_scr = rest
        wbf_ref[...] = w_ref[...].astype(BF16)
        w_ref = wbf_ref
    else:
        h_scr, cb_scr, vext_scr = rest

    @pl.when(j == 0)
    def _():
        h_scr[...] = _rms(x_ref[...], g_ref[...]).astype(BF16)

    def proj():
        return jnp.dot(h_scr[...], w_ref[...], preferred_element_type=F32)

    @pl.when(j == 0)
    def _():
        cb_scr[...] = proj()

    @pl.when(j == 1)
    def _():
        vext_scr[pl.ds(pad, tm), :] = proj()

    @pl.when(j == 2)
    def _():
        vext_scr[pl.ds(pad, tm), :] = vext_scr[pl.ds(pad, tm), :] * proj()

        @pl.when(i % tiles_per_seq == 0)
        def _():
            vext_scr[pl.ds(pad - 2 * shift, 2 * shift), :] = halo_ref[0]

    @pl.when(j == 3)
    def _():
        def conv_rows(n):
            return (cw_ref[0:1, :] * vext_scr[pl.ds(pad - 2 * shift, n), :]
                    + cw_ref[1:2, :] * vext_scr[pl.ds(pad - shift, n), :]
                    + cw_ref[2:3, :] * vext_scr[pl.ds(pad, n), :])

        z = _silu(proj())
        if shift % 8 == 0:
            act_ref[...] = (cb_scr[...] * conv_rows(tm) * z).astype(BF16)
        else:
            v = vext_scr[pl.ds(pad, tm), :]
            conv = (cw_ref[0:1, :] * pltpu.roll(v, 2 * shift, axis=0)
                    + cw_ref[1:2, :] * pltpu.roll(v, shift, axis=0) + cw_ref[2:3, :] * v)
            act_ref[...] = (cb_scr[...] * conv * z).astype(BF16)
            head = 16
            act_ref[0:head, :] = (cb_scr[0:head, :] * conv_rows(head) * z[0:head]).astype(BF16)
        tail = vext_scr[pl.ds(pad + tm - 2 * shift, 2 * shift), :]
        nconv_ref[0] = tail
        vext_scr[pl.ds(pad - 2 * shift, 2 * shift), :] = tail

    @pl.when(j == 4)
    def _():
        act_ref[...] = proj().astype(BF16)

    @pl.when(j == 6)
    def _():
        act_ref[...] = (proj() * (HEAD_DIM ** -0.5)).astype(BF16)

    @pl.when((j == 5) | (j == 7))
    def _():
        act_ref[...] = _silu(proj()).astype(BF16)

    @pl.when(j >= 8)
    def _():
        act_ref[...] = _sigmoid(proj()).astype(BF16)


def _inproj(x2d, g, w, halo, conv_w, *, tm, shift, pad, tiles_per_seq):
    rows = x2d.shape[0]
    n_seq = rows // (tm * tiles_per_seq)
    emit_w = w.dtype == F32
    assert not emit_w or rows == tm
    kern = functools.partial(_inproj_kernel, tm=tm, shift=shift, pad=pad,
                             tiles_per_seq=tiles_per_seq, emit_w=emit_w)
    w_out_spec = [pl.BlockSpec((D_MODEL, CB), lambda i, j: (0, j))] if emit_w else []
    w_out_shape = [jax.ShapeDtypeStruct((D_MODEL, N_IN), BF16)] if emit_w else []
    return pl.pallas_call(
        kern,
        grid=(rows // tm, N_IN // CB),
        in_specs=[
            pl.BlockSpec((tm, D_MODEL), lambda i, j: (i, 0)),
            pl.BlockSpec((1, D_MODEL), lambda i, j: (0, 0)),
            pl.BlockSpec((D_MODEL, CB), lambda i, j: (0, j)),
            pl.BlockSpec((1, 2 * shift, E_CONV), lambda i, j: (i // tiles_per_seq, 0, 0)),
            pl.BlockSpec((CONV_W, E_CONV), lambda i, j: (0, 0)),
        ],
        out_specs=[
            pl.BlockSpec((tm, CB), lambda i, j: (i, jnp.maximum(j - 3, 0))),
            pl.BlockSpec((1, 2 * shift, E_CONV), lambda i, j: (i // tiles_per_seq, 0, 0)),
        ] + w_out_spec,
        out_shape=[
            jax.ShapeDtypeStruct((rows, N_ACT_BLOCKS * CB), BF16),
            jax.ShapeDtypeStruct((n_seq, 2 * shift, E_CONV), F32),
        ] + w_out_shape,
        scratch_shapes=[
            pltpu.VMEM((tm, D_MODEL), BF16),
            pltpu.VMEM((tm, E_CONV), F32),
            pltpu.VMEM((pad + tm, E_CONV), F32),
        ],
        compiler_params=_cparams(("arbitrary", "arbitrary")),
        name="inproj",
    )(x2d, g, w, halo, conv_w)


def _memproj_kernel(x_ref, g_ref, w_ref, o_ref):
    h = _rms(x_ref[...], g_ref[...]).astype(BF16)
    o_ref[...] = jnp.dot(h, w_ref[...].astype(BF16), preferred_element_type=F32)


def _memproj(mem2d, g, w_bf):
    rows = mem2d.shape[0]
    tm = 512
    return pl.pallas_call(
        _memproj_kernel,
        grid=(rows // tm,),
        in_specs=[
            pl.BlockSpec((tm, D_MODEL), lambda i: (i, 0)),
            pl.BlockSpec((1, D_MODEL), lambda i: (0, 0)),
            pl.BlockSpec((D_MODEL, E_ATTN), lambda i: (0, 0)),
        ],
        out_specs=pl.BlockSpec((tm, E_ATTN), lambda i: (i, 0)),
        out_shape=jax.ShapeDtypeStruct((rows, E_ATTN), F32),
        compiler_params=_cparams(("arbitrary",)),
        name="memproj",
    )(mem2d, g, w_bf)


def _disc_kernel(lr_ref, li_ref, ldt_ref, bre_ref, bim_ref,
                 ar_ref, ai_ref, pr_ref, pi_ref, bbr_ref, bbi_ref, *, n_sq):
    lr = lr_ref[...]
    li = li_ref[...]
    dt = jnp.exp(ldt_ref[...])
    mag = jnp.exp(lr * dt)
    ang = li * dt
    ar = mag * jnp.cos(ang)
    ai = mag * jnp.sin(ang)
    den = lr * lr + li * li
    fr = ((ar - 1.0) * lr + ai * li) / den
    fi = (ai * lr - (ar - 1.0) * li) / den
    bre = bre_ref[...]
    bim = bim_ref[...]
    bbr_ref[...] = fr * bre - fi * bim
    bbi_ref[...] = fr * bim + fi * bre
    ar_ref[...] = ar
    ai_ref[...] = ai
    pr, pi = ar, ai
    for _ in range(n_sq):
        pr, pi = pr * pr - pi * pi, 2.0 * pr * pi
    pr_ref[...] = pr
    pi_ref[...] = pi


def _discretise(lam_re, lam_im, log_dt, b_re, b_im, n_sq):
    rows = N_GROUPS * GROUP
    rep = lambda a: jnp.repeat(a, GROUP, axis=0)
    lr = rep(lam_re)
    li = rep(lam_im)
    ldt = rep(jnp.broadcast_to(log_dt[:, None], (N_GROUPS, P_STATE)))
    bre_t = jnp.transpose(b_re, (0, 2, 1)).reshape(rows, P_STATE)
    bim_t = jnp.transpose(b_im, (0, 2, 1)).reshape(rows, P_STATE)
    sds = jax.ShapeDtypeStruct((rows, P_STATE), F32)
    outs = pl.pallas_call(
        functools.partial(_disc_kernel, n_sq=n_sq),
        out_shape=[sds] * 6,
        name="s5_disc",
    )(lr, li, ldt, bre_t, bim_t)
    ar, ai, pr, pi, bbr_t, bbi_t = outs
    pick = lambda a: a.reshape(N_GROUPS, GROUP, P_STATE)[:, 0, :].reshape(N_BLK, 1, BLK_ST)
    abar = jnp.concatenate([pick(ar), pick(ai)], axis=1)
    apow = jnp.concatenate([pick(pr), pick(pi)], axis=1)
    return abar, apow, bbr_t, bbi_t


def _block_diag_weights(bbr_t, bbi_t, c_re, c_im):
    st = jnp.arange(BLK_ST)
    spread = (jnp.arange(P_STATE)[:, None] == st[None, :] % P_STATE).astype(F32)
    same_group = (jnp.arange(BLK_IN)[:, None] // GROUP == st[None, :] // P_STATE).astype(F32)
    hi = lax.Precision.HIGHEST

    def b_side(a):
        a = a.reshape(N_BLK, BLK_IN, P_STATE)
        return jnp.einsum('bkp,pn->bkn', a, spread, precision=hi) * same_group

    def c_side(a):
        a = a.reshape(N_BLK, BLK_IN, P_STATE)
        return jnp.einsum('pn,bkp->bnk', spread, a, precision=hi) * same_group.T

    bblk = jnp.concatenate([b_side(bbr_t), b_side(bbi_t)], axis=2).astype(BF16)
    cblk = jnp.concatenate([c_side(c_re), c_side(-c_im)], axis=1).astype(BF16)
    return bblk, cblk


def _cmul_add(ar, ai, sr, si, xr, xi):
    return ar * sr - ai * si + xr, ar * si + ai * sr + xi


def _ssm_prompt_kernel(u_ref, sz_ref, bblk_ref, cblk_ref, abar_ref, apow_ref, d_ref,
                       ys_ref, sre_ref, sim_ref,
                       x_scr, s_scr, y_scr, ynat_scr, carry_scr, perm_scr, *, tq):
    i = pl.program_id(1)
    seg_len = tq // SEGS
    n_half = BLK_ST // SCAN_W

    @pl.when(i == 0)
    def _():
        carry_scr[...] = jnp.zeros_like(carry_scr)

    @pl.when((pl.program_id(0) == 0) & (i == 0))
    def _():
        r = lax.broadcasted_iota(jnp.int32, (tq, tq), 0)
        c = lax.broadcasted_iota(jnp.int32, (tq, tq), 1)
        perm_scr[...] = jnp.where(c == (r % SEGS) * seg_len + r // SEGS, 1.0, 0.0).astype(BF16)

    u = u_ref[...]
    u_perm = jnp.dot(perm_scr[...], u, preferred_element_type=F32).astype(BF16)
    row_id = lax.broadcasted_iota(jnp.int32, (SEGS, SCAN_W), 0)

    for blk in range(N_BLK):
        par = blk % 2
        x_scr[par] = jnp.dot(u_perm[:, blk * BLK_IN:(blk + 1) * BLK_IN], bblk_ref[blk],
                             preferred_element_type=F32)
        for half in range(n_half):
            lo = half * SCAN_W
            re_cols = slice(lo, lo + SCAN_W)
            im_cols = slice(BLK_ST + lo, BLK_ST + lo + SCAN_W)
            ar = jnp.broadcast_to(abar_ref[blk, 0:1, lo:lo + SCAN_W], (SEGS, SCAN_W))
            ai = jnp.broadcast_to(abar_ref[blk, 1:2, lo:lo + SCAN_W], (SEGS, SCAN_W))

            def local_end(j, carry, par=par, re_cols=re_cols, im_cols=im_cols, ar=ar, ai=ai):
                row = pl.multiple_of(j * SEGS, SEGS)
                xr = x_scr[par, pl.ds(row, SEGS), re_cols]
                xi = x_scr[par, pl.ds(row, SEGS), im_cols]
                return _cmul_add(ar, ai, carry[0], carry[1], xr, xi)

            zero = jnp.zeros((SEGS, SCAN_W), F32)
            er, ei = lax.fori_loop(0, seg_len, local_end, (zero, zero), unroll=True)

            pr = apow_ref[blk, 0:1, lo:lo + SCAN_W]
            pi = apow_ref[blk, 1:2, lo:lo + SCAN_W]
            prev_r = carry_scr[blk, 0:1, lo:lo + SCAN_W]
            prev_i = carry_scr[blk, 1:2, lo:lo + SCAN_W]
            init_r = zero
            init_i = zero
            for k in range(SEGS):
                init_r = jnp.where(row_id == k, prev_r, init_r)
                init_i = jnp.where(row_id == k, prev_i, init_i)
                prev_r, prev_i = _cmul_add(pr, pi, prev_r, prev_i,
                                           er[k:k + 1, :], ei[k:k + 1, :])
            carry_scr[blk, 0:1, lo:lo + SCAN_W] = prev_r
            carry_scr[blk, 1:2, lo:lo + SCAN_W] = prev_i

            def full_scan(j, carry, par=par, re_cols=re_cols, im_cols=im_cols, ar=ar, ai=ai):
                row = pl.multiple_of(j * SEGS, SEGS)
                xr = x_scr[par, pl.ds(row, SEGS), re_cols]
                xi = x_scr[par, pl.ds(row, SEGS), im_cols]
                sr, si = _cmul_add(ar, ai, carry[0], carry[1], xr, xi)
                s_scr[par, pl.ds(row, SEGS), re_cols] = sr
                s_scr[par, pl.ds(row, SEGS), im_cols] = si
                return sr, si

            lax.fori_loop(0, seg_len, full_scan, (init_r, init_i), unroll=True)

        y = jnp.dot(s_scr[par].astype(BF16), cblk_ref[blk], preferred_element_type=F32)
        for sl in range(BLK_IN // 128):
            y_scr[blk * (BLK_IN // 128) + sl] = y[:, sl * 128:(sl + 1) * 128]

    for slab in range(E_SSM // 128):
        for k in range(SEGS):
            ynat_scr[k * seg_len:(k + 1) * seg_len, slab * 128:(slab + 1) * 128] = (
                y_scr[slab, pl.ds(k, seg_len, stride=SEGS), :])

    yy = (ynat_scr[...] + d_ref[...] * u.astype(F32)) * sz_ref[...].astype(F32)
    ys_ref[...] = jax.nn.gelu(yy).astype(BF16)
    for blk in range(N_BLK):
        sre_ref[0, :, blk * BLK_ST:(blk + 1) * BLK_ST] = carry_scr[blk, 0:1, :]
        sim_ref[0, :, blk * BLK_ST:(blk + 1) * BLK_ST] = carry_scr[blk, 1:2, :]


def _ssm_prompt(act, bblk, cblk, abar, apow, d_skip, *, tq):
    tiles = SEQ // tq
    n_half = BLK_ST // SCAN_W
    st = jax.ShapeDtypeStruct((BATCH, 1, N_GROUPS * P_STATE), F32)
    const3 = lambda b, i: (0, 0, 0)
    return pl.pallas_call(
        functools.partial(_ssm_prompt_kernel, tq=tq),
        grid=(BATCH, tiles),
        in_specs=[
            pl.BlockSpec((tq, E_SSM), lambda b, i: (b * tiles + i, 1)),
            pl.BlockSpec((tq, E_SSM), lambda b, i: (b * tiles + i, 2)),
            pl.BlockSpec((N_BLK, BLK_IN, 2 * BLK_ST), const3),
            pl.BlockSpec((N_BLK, 2 * BLK_ST, BLK_IN), const3),
            pl.BlockSpec((N_BLK, 2, BLK_ST), const3),
            pl.BlockSpec((N_BLK, 2, BLK_ST), const3),
            pl.BlockSpec((1, E_SSM), lambda b, i: (0, 0)),
        ],
        out_specs=[
            pl.BlockSpec((tq, E_SSM), lambda b, i: (b * tiles + i, 0)),
            pl.BlockSpec((1, 1, N_GROUPS * P_STATE), lambda b, i: (b, 0, 0)),
            pl.BlockSpec((1, 1, N_GROUPS * P_STATE), lambda b, i: (b, 0, 0)),
        ],
        out_shape=[jax.ShapeDtypeStruct((BATCH * SEQ, E_SSM), BF16), st, st],
        scratch_shapes=[
            pltpu.VMEM((2, tq, 2 * BLK_ST), F32),
            pltpu.VMEM((2, tq, 2 * BLK_ST), F32),
            pltpu.VMEM((E_SSM // 128, tq, 128), F32),
            pltpu.VMEM((tq, E_SSM), F32),
            pltpu.VMEM((N_BLK, 2, BLK_ST), F32),
            pltpu.VMEM((tq, tq), BF16),
        ],
        compiler_params=_cparams(("arbitrary", "arbitrary")),
        name="ssm_prompt",
    )(act, act, bblk, cblk, abar, apow, d_skip)


def _ssm_sample_kernel(u_ref, sz_ref, bblk_ref, cblk_ref, abar_ref, d_ref, s0r_ref, s0i_ref,
                       ys_ref, sre_ref, sim_ref, x_scr, s_scr):
    n_half = BLK_ST // SCAN_W
    rows = DEC_SEQ * DEC_BATCH
    u = u_ref[...]
    ys = []
    for blk in range(N_BLK):
        x = jnp.dot(u[:, blk * BLK_IN:(blk + 1) * BLK_IN], bblk_ref[blk],
                    preferred_element_type=F32)
        for cch in range(2 * n_half):
            x_scr[cch] = x[:, cch * SCAN_W:(cch + 1) * SCAN_W]
        for half in range(n_half):
            lo = half * SCAN_W
            col = blk * BLK_ST + lo
            ar = jnp.broadcast_to(abar_ref[blk, 0:1, lo:lo + SCAN_W], (8, SCAN_W))
            ai = jnp.broadcast_to(abar_ref[blk, 1:2, lo:lo + SCAN_W], (8, SCAN_W))

            def body(rc, carry, half=half, col=col, ar=ar, ai=ai):
                r0 = pl.multiple_of(rc * 8, 8)
                sr = s0r_ref[pl.ds(r0, 8), col:col + SCAN_W]
                si = s0i_ref[pl.ds(r0, 8), col:col + SCAN_W]
                for t in range(DEC_SEQ):
                    row = pl.multiple_of(t * DEC_BATCH + r0, 8)
                    xr = x_scr[half, pl.ds(row, 8), :]
                    xi = x_scr[n_half + half, pl.ds(row, 8), :]
                    sr, si = _cmul_add(ar, ai, sr, si, xr, xi)
                    s_scr[half, pl.ds(row, 8), :] = sr
                    s_scr[n_half + half, pl.ds(row, 8), :] = si
                sre_ref[pl.ds(r0, 8), col:col + SCAN_W] = sr
                sim_ref[pl.ds(r0, 8), col:col + SCAN_W] = si
                return carry

            lax.fori_loop(0, DEC_BATCH // 8, body, 0)

        y = jnp.zeros((rows, BLK_IN), F32)
        for cch in range(2 * n_half):
            y = y + jnp.dot(s_scr[cch].astype(BF16),
                            cblk_ref[blk, cch * SCAN_W:(cch + 1) * SCAN_W, :],
                            preferred_element_type=F32)
        ys.append(y)
    y_all = jnp.concatenate(ys, axis=1)
    yy = (y_all + d_ref[...] * u.astype(F32)) * sz_ref[...].astype(F32)
    ys_ref[...] = jax.nn.gelu(yy).astype(BF16)


def _ssm_sample(act, bblk, cblk, abar, d_skip, s0r, s0i):
    rows = DEC_SEQ * DEC_BATCH
    n_half = BLK_ST // SCAN_W
    st = jax.ShapeDtypeStruct((DEC_BATCH, N_GROUPS * P_STATE), F32)
    full = lambda shape: pl.BlockSpec(shape, lambda i: tuple(0 for _ in shape))
    return pl.pallas_call(
        _ssm_sample_kernel,
        grid=(1,),
        in_specs=[
            pl.BlockSpec((rows, E_SSM), lambda i: (0, 1)),
            pl.BlockSpec((rows, E_SSM), lambda i: (0, 2)),
            full((N_BLK, BLK_IN, 2 * BLK_ST)),
            full((N_BLK, 2 * BLK_ST, BLK_IN)),
            full((N_BLK, 2, BLK_ST)),
            full((1, E_SSM)),
            full((DEC_BATCH, N_GROUPS * P_STATE)),
            full((DEC_BATCH, N_GROUPS * P_STATE)),
        ],
        out_specs=[full((rows, E_SSM)), full((DEC_BATCH, N_GROUPS * P_STATE)),
                   full((DEC_BATCH, N_GROUPS * P_STATE))],
        out_shape=[jax.ShapeDtypeStruct((rows, E_SSM), BF16), st, st],
        scratch_shapes=[
            pltpu.VMEM((2 * n_half, rows, SCAN_W), F32),
            pltpu.VMEM((2 * n_half, rows, SCAN_W), F32),
        ],
        compiler_params=_cparams(("arbitrary",)),
        name="ssm_sample",
    )(act, act, bblk, cblk, abar, d_skip, s0r, s0i)


def _softmax_rows(s):
    m = jnp.max(s, axis=-1, keepdims=True)
    e = jnp.exp(s - m)
    return e / jnp.sum(e, axis=-1, keepdims=True)


def _attn_prompt_kernel(q_ref, az_ref, k_ref, v_ref, o_ref):
    for h in range(N_HEADS):
        cols = slice(h * HEAD_DIM, (h + 1) * HEAD_DIM)
        kh = k_ref[:, cols].astype(BF16)
        vh = v_ref[:, cols].astype(BF16)
        s = lax.dot_general(q_ref[:, cols], kh, (((1,), (1,)), ((), ())),
                            preferred_element_type=F32)
        p = _softmax_rows(s).astype(BF16)
        o = jnp.dot(p, vh, preferred_element_type=F32)
        o_ref[:, cols] = (o * az_ref[:, cols].astype(F32)).astype(BF16)


def _attn_prompt(act, mk2d, mv2d, *, tr):
    tiles = SEQ // tr
    return pl.pallas_call(
        _attn_prompt_kernel,
        grid=(BATCH, tiles),
        in_specs=[
            pl.BlockSpec((tr, E_ATTN), lambda b, i: (b * tiles + i, 3)),
            pl.BlockSpec((tr, E_ATTN), lambda b, i: (b * tiles + i, 4)),
            pl.BlockSpec((MEM_LEN, E_ATTN), lambda b, i: (b, 0)),
            pl.BlockSpec((MEM_LEN, E_ATTN), lambda b, i: (b, 0)),
        ],
        out_specs=pl.BlockSpec((tr, E_ATTN), lambda b, i: (b * tiles + i, 0)),
        out_shape=jax.ShapeDtypeStruct((BATCH * SEQ, E_ATTN), BF16),
        compiler_params=_cparams(("arbitrary", "arbitrary")),
        name="attn_prompt",
    )(act, act, mk2d, mv2d)


def _attn_sample_kernel(q_ref, az_ref, k_ref, v_ref, o_ref, *, bt):
    n_kv = bt * MEM_LEN * N_HEADS
    k = k_ref[...].reshape(n_kv, HEAD_DIM).astype(BF16)
    v = v_ref[...].reshape(n_kv, HEAD_DIM).astype(BF16)
    rows = DEC_SEQ * bt
    q = jnp.concatenate(
        [q_ref[:, :, h * HEAD_DIM:(h + 1) * HEAD_DIM].reshape(rows, HEAD_DIM)
         for h in range(N_HEADS)], axis=0)
    s = lax.dot_general(q, k, (((1,), (1,)), ((), ())), preferred_element_type=F32)
    row = lax.broadcasted_iota(jnp.int32, s.shape, 0)
    col = lax.broadcasted_iota(jnp.int32, s.shape, 1)
    same = ((row // rows == col % N_HEADS)
            & (row % bt == col // (MEM_LEN * N_HEADS)))
    s = jnp.where(same, s, -1e30)
    p = _softmax_rows(s).astype(BF16)
    o = jnp.dot(p, v, preferred_element_type=F32)
    for h in range(N_HEADS):
        cols = slice(h * HEAD_DIM, (h + 1) * HEAD_DIM)
        az = az_ref[:, :, cols].reshape(rows, HEAD_DIM).astype(F32)
        o_ref[:, :, cols] = (o[h * rows:(h + 1) * rows] * az).astype(BF16).reshape(
            DEC_SEQ, bt, HEAD_DIM)


def _attn_sample(act3, k4, v4, *, bt):
    kv_spec = pl.BlockSpec((bt, MEM_LEN, N_HEADS, HEAD_DIM), lambda b: (b, 0, 0, 0))
    return pl.pallas_call(
        functools.partial(_attn_sample_kernel, bt=bt),
        grid=(DEC_BATCH // bt,),
        in_specs=[
            pl.BlockSpec((DEC_SEQ, bt, E_ATTN), lambda b: (0, b, 3)),
            pl.BlockSpec((DEC_SEQ, bt, E_ATTN), lambda b: (0, b, 4)),
            kv_spec, kv_spec,
        ],
        out_specs=pl.BlockSpec((DEC_SEQ, bt, E_ATTN), lambda b: (0, b, 0)),
        out_shape=jax.ShapeDtypeStruct((DEC_SEQ, DEC_BATCH, E_ATTN), BF16),
        compiler_params=_cparams(("arbitrary",)),
        name="attn_sample",
    )(act3, act3, k4, v4)


def _merge_kernel(gated_ref, ys_ref, oz_ref, g0_ref, g1_ref, g2_ref,
                  wco_ref, wga_ref, wgb_ref, wao_ref, o_ref, w_scr):
    @pl.when(pl.program_id(1) == 0)
    def _():
        for k, w_ref in enumerate((wco_ref, wga_ref, wgb_ref, wao_ref)):
            w_scr[k] = w_ref[...].astype(BF16)

    dot = functools.partial(jnp.dot, preferred_element_type=F32)
    conv_out = dot(gated_ref[...], w_scr[0])
    ys = ys_ref[...]
    ssm_out = dot(ys, w_scr[1]) * _sigmoid(dot(ys, w_scr[2]))
    attn_out = dot(oz_ref[...], w_scr[3])
    merged = (g0_ref[...].astype(F32) * conv_out + g1_ref[...].astype(F32) * ssm_out
              + g2_ref[...].astype(F32) * attn_out)
    o_ref[...] = merged.astype(BF16)


def _merge(act, ys, oz, wco, wga, wgb, wao, *, tm):
    rows = act.shape[0]
    tn = 1024
    nn = D_MODEL // tn
    wspec = pl.BlockSpec((E_CONV, tn), lambda n, i: (0, n), pipeline_mode=pl.Buffered(1))
    return pl.pallas_call(
        _merge_kernel,
        grid=(nn, rows // tm),
        in_specs=[
            pl.BlockSpec((tm, CB), lambda n, i: (i, 0)),
            pl.BlockSpec((tm, E_SSM), lambda n, i: (i, 0)),
            pl.BlockSpec((tm, E_ATTN), lambda n, i: (i, 0)),
            pl.BlockSpec((tm, tn), lambda n, i: (i, 5 + n)),
            pl.BlockSpec((tm, tn), lambda n, i: (i, 5 + nn + n)),
            pl.BlockSpec((tm, tn), lambda n, i: (i, 5 + 2 * nn + n)),
            wspec, wspec, wspec, wspec,
        ],
        out_specs=pl.BlockSpec((tm, tn), lambda n, i: (i, n)),
        out_shape=jax.ShapeDtypeStruct((rows, D_MODEL), BF16),
        scratch_shapes=[pltpu.VMEM((4, E_CONV, tn), BF16)],
        compiler_params=_cparams(("arbitrary", "arbitrary")),
        name="merge",
    )(act, ys, oz, act, act, act, wco, wga, wgb, wao)


def _out_kernel(x_ref, m_ref, w_ref, g_ref, o_ref):
    y = x_ref[...] + jnp.dot(m_ref[...], w_ref[...], preferred_element_type=F32)
    o_ref[...] = _rms(y, g_ref[...])


def _outproj(x2d, merged, w_bf, g, *, tm):
    rows = x2d.shape[0]
    return pl.pallas_call(
        _out_kernel,
        grid=(rows // tm,),
        in_specs=[
            pl.BlockSpec((tm, D_MODEL), lambda i: (i, 0)),
            pl.BlockSpec((tm, D_MODEL), lambda i: (i, 0)),
            pl.BlockSpec((D_MODEL, D_MODEL), lambda i: (0, 0)),
            pl.BlockSpec((1, D_MODEL), lambda i: (0, 0)),
        ],
        out_specs=pl.BlockSpec((tm, D_MODEL), lambda i: (i, 0)),
        out_shape=jax.ShapeDtypeStruct((rows, D_MODEL), F32),
        compiler_params=_cparams(("arbitrary",)),
        name="outproj",
    )(x2d, merged, w_bf, g)


def kernel(x_prompt, x_sample, mem_prompt, cache_mem_k, cache_mem_v, state_conv, state_ssm_re, state_ssm_im, norm_g, mem_norm_g, w_in, conv_w, w_conv_out, ssm_lambda_re, ssm_lambda_im, ssm_log_dt, ssm_b_re, ssm_b_im, ssm_c_re, ssm_c_im, ssm_d, w_glu_a, w_glu_b, w_mem_k, w_mem_v, w_attn_out, w_out, final_norm_g):
    l = 0
    tq = 512
    bf = lambda w: w.astype(BF16)
    wco, wga, wgb, wao, wo = (w_conv_out[l], w_glu_a[l], w_glu_b[l], w_attn_out[l], bf(w_out[l]))
    g_in = norm_g[l][None, :]
    g_fin = final_norm_g[None, :]
    d_skip = ssm_d[l][None, :]

    n_sq = int(math.log2(tq // SEGS))
    abar, apow, bbr_t, bbi_t = _discretise(ssm_lambda_re[l], ssm_lambda_im[l], ssm_log_dt[l],
                                           ssm_b_re[l], ssm_b_im[l], n_sq)
    bblk, cblk = _block_diag_weights(bbr_t, bbi_t, ssm_c_re[l], ssm_c_im[l])

    rows_s = DEC_SEQ * DEC_BATCH
    xs = jnp.transpose(x_sample, (1, 0, 2)).reshape(rows_s, D_MODEL)
    halo_s = jnp.transpose(state_conv[l], (1, 0, 2)).reshape(1, 2 * DEC_BATCH, E_CONV)
    act_s, nconv_s, w_in_bf = _inproj(xs, g_in, w_in[l], halo_s, conv_w[l], tm=rows_s,
                                      shift=DEC_BATCH, pad=2 * DEC_BATCH, tiles_per_seq=1)

    xp = x_prompt.reshape(BATCH * SEQ, D_MODEL)
    tm_p = 1024
    act_p, nconv_p = _inproj(xp, g_in, w_in_bf, jnp.zeros((BATCH, 2, E_CONV), F32), conv_w[l],
                             tm=tm_p, shift=1, pad=8, tiles_per_seq=SEQ // tm_p)
    mem2d = mem_prompt.reshape(BATCH * MEM_LEN, D_MODEL)
    g_mem = mem_norm_g[l][None, :]
    mk = _memproj(mem2d, g_mem, w_mem_k[l])
    mv = _memproj(mem2d, g_mem, w_mem_v[l])
    ys_p, sre_p, sim_p = _ssm_prompt(act_p, bblk, cblk, abar, apow, d_skip, tq=tq)
    oz_p = _attn_prompt(act_p, mk, mv, tr=512)
    merged_p = _merge(act_p, ys_p, oz_p, wco, wga, wgb, wao, tm=512)
    y_p = _outproj(xp, merged_p, wo, g_fin, tm=512)

    s0r = state_ssm_re[l].reshape(DEC_BATCH, N_GROUPS * P_STATE)
    s0i = state_ssm_im[l].reshape(DEC_BATCH, N_GROUPS * P_STATE)
    ys_s, sre_s, sim_s = _ssm_sample(act_s, bblk, cblk, abar, d_skip, s0r, s0i)
    k3 = cache_mem_k[l]
    v3 = cache_mem_v[l]
    oz_s = _attn_sample(act_s.reshape(DEC_SEQ, DEC_BATCH, N_ACT_BLOCKS * CB), k3, v3, bt=8)
    merged_s = _merge(act_s, ys_s, oz_s.reshape(rows_s, E_ATTN), wco, wga, wgb, wao, tm=512)
    y_s = _outproj(xs, merged_s, wo, g_fin, tm=512)

    y_prompt = y_p.reshape(BATCH, SEQ, D_MODEL)
    y_sample = jnp.transpose(y_s.reshape(DEC_SEQ, DEC_BATCH, D_MODEL), (1, 0, 2))
    st_shape_p = (1, BATCH, N_GROUPS, P_STATE)
    st_shape_s = (1, DEC_BATCH, N_GROUPS, P_STATE)
    new_conv_s = jnp.transpose(nconv_s.reshape(2, DEC_BATCH, E_CONV), (1, 0, 2))
    return (y_prompt, y_sample,
            mk.reshape(1, BATCH, MEM_LEN, N_HEADS, HEAD_DIM),
            mv.reshape(1, BATCH, MEM_LEN, N_HEADS, HEAD_DIM),
            nconv_p[None],
            sre_p.reshape(st_shape_p), sim_p.reshape(st_shape_p),
            new_conv_s[None],
            sre_s.reshape(st_shape_s), sim_s.reshape(st_shape_s))
```

```python
import functools
import math

import jax
import jax.numpy as jnp
from jax import lax
from jax.experimental import pallas as pl
from jax.experimental.pallas import tpu as pltpu

D_MODEL = 2048
BATCH = 4
SEQ = 2048
DEC_BATCH = 128
DEC_SEQ = 4
E_CONV = 1024
CONV_W = 3
E_SSM = 1024
GROUP = 16
N_GROUPS = 64
P_STATE = 64
N_HEADS = 4
HEAD_DIM = 256
E_ATTN = 1024
MEM_LEN = 256
N_IN = 14336
EPS = 1e-6

F32 = jnp.float32
BF16 = jnp.bfloat16

CB = 1024
N_ACT_BLOCKS = 11
GROUPS_PER_BLK = 16
N_BLK = N_GROUPS // GROUPS_PER_BLK
BLK_IN = GROUPS_PER_BLK * GROUP
BLK_ST = GROUPS_PER_BLK * P_STATE
SCAN_W = 512
SEGS = 8
VMEM_LIMIT = 56 * 1024 * 1024


def _cparams(sem):
    return pltpu.CompilerParams(dimension_semantics=sem, vmem_limit_bytes=VMEM_LIMIT)


def _sigmoid(x):
    return 0.5 * jnp.tanh(0.5 * x) + 0.5


def _silu(x):
    return x * _sigmoid(x)


def _rms(x, g):
    ms = jnp.mean(x * x, axis=-1, keepdims=True)
    return x * lax.rsqrt(ms + EPS) * g


def _inproj_kernel(x_ref, g_ref, w_ref, halo_ref, cw_ref, act_ref, nconv_ref, *rest,
                   tm, shift, pad, tiles_per_seq, emit_w):
    i = pl.program_id(0)
    j = pl.program_id(1)
    if emit_w:
        wbf_ref, h_scr, cb_scr, vext_scr = rest
        wbf_ref[...] = w_ref[...].astype(BF16)
        w_ref = wbf_ref
    else:
        h_scr, cb_scr, vext_scr = rest

    @pl.when(j == 0)
    def _():
        h_scr[...] = _rms(x_ref[...], g_ref[...]).astype(BF16)

    def proj():
        return jnp.dot(h_scr[...], w_ref[...], preferred_element_type=F32)

    @pl.when(j == 0)
    def _():
        cb_scr[...] = proj()

    @pl.when(j == 1)
    def _():
        vext_scr[pl.ds(pad, tm), :] = proj()

    @pl.when(j == 2)
    def _():
        vext_scr[pl.ds(pad, tm), :] = vext_scr[pl.ds(pad, tm), :] * proj()

        @pl.when(i % tiles_per_seq == 0)
        def _():
            vext_scr[pl.ds(pad - 2 * shift, 2 * shift), :] = halo_ref[0]

    @pl.when(j == 3)
    def _():
        def conv_rows(n):
            return (cw_ref[0:1, :] * vext_scr[pl.ds(pad - 2 * shift, n), :]
                    + cw_ref[1:2, :] * vext_scr[pl.ds(pad - shift, n), :]
                    + cw_ref[2:3, :] * vext_scr[pl.ds(pad, n), :])

        z = _silu(proj())
        if shift % 8 == 0:
            act_ref[...] = (cb_scr[...] * conv_rows(tm) * z).astype(BF16)
        else:
            v = vext_scr[pl.ds(pad, tm), :]
            conv = (cw_ref[0:1, :] * pltpu.roll(v, 2 * shift, axis=0)
                    + cw_ref[1:2, :] * pltpu.roll(v, shift, axis=0) + cw_ref[2:3, :] * v)
            act_ref[...] = (cb_scr[...] * conv * z).astype(BF16)
            head = 16
            act_ref[0:head, :] = (cb_scr[0:head, :] * conv_rows(head) * z[0:head]).astype(BF16)
        tail = vext_scr[pl.ds(pad + tm - 2 * shift, 2 * shift), :]
        nconv_ref[0] = tail
        vext_scr[pl.ds(pad - 2 * shift, 2 * shift), :] = tail

    @pl.when(j == 4)
    def _():
        act_ref[...] = proj().astype(BF16)

    @pl.when(j == 6)
    def _():
        act_ref[...] = (proj() * (HEAD_DIM ** -0.5)).astype(BF16)

    @pl.when((j == 5) | (j == 7))
    def _():
        act_ref[...] = _silu(proj()).astype(BF16)

    @pl.when(j >= 8)
    def _():
        act_ref[...] = _sigmoid(proj()).astype(BF16)


def _inproj(x2d, g, w, halo, conv_w, *, tm, shift, pad, tiles_per_seq):
    rows = x2d.shape[0]
    n_seq = rows // (tm * tiles_per_seq)
    emit_w = w.dtype == F32
    assert not emit_w or rows == tm
    kern = functools.partial(_inproj_kernel, tm=tm, shift=shift, pad=pad,
                             tiles_per_seq=tiles_per_seq, emit_w=emit_w)
    w_out_spec = [pl.BlockSpec((D_MODEL, CB), lambda i, j: (0, j))] if emit_w else []
    w_out_shape = [jax.ShapeDtypeStruct((D_MODEL, N_IN), BF16)] if emit_w else []
    return pl.pallas_call(
        kern,
        grid=(rows // tm, N_IN // CB),
        in_specs=[
            pl.BlockSpec((tm, D_MODEL), lambda i, j: (i, 0)),
            pl.BlockSpec((1, D_MODEL), lambda i, j: (0, 0)),
            pl.BlockSpec((D_MODEL, CB), lambda i, j: (0, j)),
            pl.BlockSpec((1, 2 * shift, E_CONV), lambda i, j: (i // tiles_per_seq, 0, 0)),
            pl.BlockSpec((CONV_W, E_CONV), lambda i, j: (0, 0)),
        ],
        out_specs=[
            pl.BlockSpec((tm, CB), lambda i, j: (i, jnp.maximum(j - 3, 0))),
            pl.BlockSpec((1, 2 * shift, E_CONV), lambda i, j: (i // tiles_per_seq, 0, 0)),
        ] + w_out_spec,
        out_shape=[
            jax.ShapeDtypeStruct((rows, N_ACT_BLOCKS * CB), BF16),
            jax.ShapeDtypeStruct((n_seq, 2 * shift, E_CONV), F32),
        ] + w_out_shape,
        scratch_shapes=[
            pltpu.VMEM((tm, D_MODEL), BF16),
            pltpu.VMEM((tm, E_CONV), F32),
            pltpu.VMEM((pad + tm, E_CONV), F32),
        ],
        compiler_params=_cparams(("arbitrary", "arbitrary")),
        name="inproj",
    )(x2d, g, w, halo, conv_w)


def _memproj_kernel(x_ref, g_ref, wk_ref, wv_ref, k_ref, v_ref):
    h = _rms(x_ref[...], g_ref[...]).astype(BF16)
    k_ref[...] = jnp.dot(h, wk_ref[...].astype(BF16), preferred_element_type=F32)
    v_ref[...] = jnp.dot(h, wv_ref[...].astype(BF16), preferred_element_type=F32)


def _memproj(mem2d, g, wk, wv):
    rows = mem2d.shape[0]
    tm = 512
    w_spec = pl.BlockSpec((D_MODEL, E_ATTN), lambda i: (0, 0), pipeline_mode=pl.Buffered(1))
    o_spec = pl.BlockSpec((tm, E_ATTN), lambda i: (i, 0))
    o_shape = jax.ShapeDtypeStruct((rows, E_ATTN), F32)
    return pl.pallas_call(
        _memproj_kernel,
        grid=(rows // tm,),
        in_specs=[
            pl.BlockSpec((tm, D_MODEL), lambda i: (i, 0)),
            pl.BlockSpec((1, D_MODEL), lambda i: (0, 0)),
            w_spec, w_spec,
        ],
        out_specs=[o_spec, o_spec],
        out_shape=[o_shape, o_shape],
        compiler_params=_cparams(("arbitrary",)),
        name="memproj",
    )(mem2d, g, wk, wv)


def _disc_kernel(lr_ref, li_ref, ldt_ref, bre_ref, bim_ref,
                 ar_ref, ai_ref, pr_ref, pi_ref, bbr_ref, bbi_ref, *, n_sq):
    lr = lr_ref[...]
    li = li_ref[...]
    dt = jnp.exp(ldt_ref[...])
    mag = jnp.exp(lr * dt)
    ang = li * dt
    ar = mag * jnp.cos(ang)
    ai = mag * jnp.sin(ang)
    den = lr * lr + li * li
    fr = ((ar - 1.0) * lr + ai * li) / den
    fi = (ai * lr - (ar - 1.0) * li) / den
    bre = bre_ref[...]
    bim = bim_ref[...]
    bbr_ref[...] = fr * bre - fi * bim
    bbi_ref[...] = fr * bim + fi * bre
    ar_ref[...] = ar
    ai_ref[...] = ai
    pr, pi = ar, ai
    for _ in range(n_sq):
        pr, pi = pr * pr - pi * pi, 2.0 * pr * pi
    pr_ref[...] = pr
    pi_ref[...] = pi


def _discretise(lam_re, lam_im, log_dt, b_re, b_im, n_sq):
    rows = N_GROUPS * GROUP
    rep = lambda a: jnp.repeat(a, GROUP, axis=0)
    lr = rep(lam_re)
    li = rep(lam_im)
    ldt = rep(jnp.broadcast_to(log_dt[:, None], (N_GROUPS, P_STATE)))
    bre_t = jnp.transpose(b_re, (0, 2, 1)).reshape(rows, P_STATE)
    bim_t = jnp.transpose(b_im, (0, 2, 1)).reshape(rows, P_STATE)
    sds = jax.ShapeDtypeStruct((rows, P_STATE), F32)
    outs = pl.pallas_call(
        functools.partial(_disc_kernel, n_sq=n_sq),
        out_shape=[sds] * 6,
        name="s5_disc",
    )(lr, li, ldt, bre_t, bim_t)
    ar, ai, pr, pi, bbr_t, bbi_t = outs
    pick = lambda a: a.reshape(N_GROUPS, GROUP, P_STATE)[:, 0, :].reshape(N_BLK, 1, BLK_ST)
    abar = jnp.concatenate([pick(ar), pick(ai)], axis=1)
    apow = jnp.concatenate([pick(pr), pick(pi)], axis=1)
    return abar, apow, bbr_t, bbi_t


def _block_diag_weights(bbr_t, bbi_t, c_re, c_im):
    st = jnp.arange(BLK_ST)
    spread = (jnp.arange(P_STATE)[:, None] == st[None, :] % P_STATE).astype(F32)
    same_group = (jnp.arange(BLK_IN)[:, None] // GROUP == st[None, :] // P_STATE).astype(F32)
    hi = lax.Precision.HIGHEST

    def b_side(a):
        a = a.reshape(N_BLK, BLK_IN, P_STATE)
        return jnp.einsum('bkp,pn->bkn', a, spread, precision=hi) * same_group

    def c_side(a):
        a = a.reshape(N_BLK, BLK_IN, P_STATE)
        return jnp.einsum('pn,bkp->bnk', spread, a, precision=hi) * same_group.T

    bblk = jnp.concatenate([b_side(bbr_t), b_side(bbi_t)], axis=2).astype(BF16)
    cblk = jnp.concatenate([c_side(c_re), c_side(-c_im)], axis=1).astype(BF16)
    return bblk, cblk


def _cmul_add(ar, ai, sr, si, xr, xi):
    return ar * sr - ai * si + xr, ar * si + ai * sr + xi


def _ssm_prompt_kernel(u_ref, sz_ref, bblk_ref, cblk_ref, abar_ref, apow_ref, d_ref,
                       ys_ref, sre_ref, sim_ref,
                       x_scr, s_scr, y_scr, ynat_scr, carry_scr, perm_scr, *, tq):
    i = pl.program_id(1)
    seg_len = tq // SEGS
    n_half = BLK_ST // SCAN_W

    @pl.when(i == 0)
    def _():
        carry_scr[...] = jnp.zeros_like(carry_scr)

    @pl.when((pl.program_id(0) == 0) & (i == 0))
    def _():
        r = lax.broadcasted_iota(jnp.int32, (tq, tq), 0)
        c = lax.broadcasted_iota(jnp.int32, (tq, tq), 1)
        perm_scr[...] = jnp.where(c == (r % SEGS) * seg_len + r // SEGS, 1.0, 0.0).astype(BF16)

    u = u_ref[...]
    u_perm = jnp.dot(perm_scr[...], u, preferred_element_type=F32).astype(BF16)
    row_id = lax.broadcasted_iota(jnp.int32, (SEGS, SCAN_W), 0)

    for blk in range(N_BLK):
        par = blk % 2
        x_scr[par] = jnp.dot(u_perm[:, blk * BLK_IN:(blk + 1) * BLK_IN], bblk_ref[blk],
                             preferred_element_type=F32)
        for half in range(n_half):
            lo = half * SCAN_W
            re_cols = slice(lo, lo + SCAN_W)
            im_cols = slice(BLK_ST + lo, BLK_ST + lo + SCAN_W)
            ar = jnp.broadcast_to(abar_ref[blk, 0:1, lo:lo + SCAN_W], (SEGS, SCAN_W))
            ai = jnp.broadcast_to(abar_ref[blk, 1:2, lo:lo + SCAN_W], (SEGS, SCAN_W))

            def local_end(j, carry, par=par, re_cols=re_cols, im_cols=im_cols, ar=ar, ai=ai):
                row = pl.multiple_of(j * SEGS, SEGS)
                xr = x_scr[par, pl.ds(row, SEGS), re_cols]
                xi = x_scr[par, pl.ds(row, SEGS), im_cols]
                return _cmul_add(ar, ai, carry[0], carry[1], xr, xi)

            zero = jnp.zeros((SEGS, SCAN_W), F32)
            er, ei = lax.fori_loop(0, seg_len, local_end, (zero, zero), unroll=True)

            pr = apow_ref[blk, 0:1, lo:lo + SCAN_W]
            pi = apow_ref[blk, 1:2, lo:lo + SCAN_W]
            prev_r = carry_scr[blk, 0:1, lo:lo + SCAN_W]
            prev_i = carry_scr[blk, 1:2, lo:lo + SCAN_W]
            init_r = zero
            init_i = zero
            for k in range(SEGS):
                init_r = jnp.where(row_id == k, prev_r, init_r)
                init_i = jnp.where(row_id == k, prev_i, init_i)
                prev_r, prev_i = _cmul_add(pr, pi, prev_r, prev_i,
                                           er[k:k + 1, :], ei[k:k + 1, :])
            carry_scr[blk, 0:1, lo:lo + SCAN_W] = prev_r
            carry_scr[blk, 1:2, lo:lo + SCAN_W] = prev_i

            def full_scan(j, carry, par=par, re_cols=re_cols, im_cols=im_cols, ar=ar, ai=ai):
                row = pl.multiple_of(j * SEGS, SEGS)
                xr = x_scr[par, pl.ds(row, SEGS), re_cols]
                xi = x_scr[par, pl.ds(row, SEGS), im_cols]
                sr, si = _cmul_add(ar, ai, carry[0], carry[1], xr, xi)
                s_scr[par, pl.ds(row, SEGS), re_cols] = sr
                s_scr[par, pl.ds(row, SEGS), im_cols] = si
                return sr, si

            lax.fori_loop(0, seg_len, full_scan, (init_r, init_i), unroll=True)

        y = jnp.dot(s_scr[par].astype(BF16), cblk_ref[blk], preferred_element_type=F32)
        for sl in range(BLK_IN // 128):
            y_scr[blk * (BLK_IN // 128) + sl] = y[:, sl * 128:(sl + 1) * 128]

    for slab in range(E_SSM // 128):
        for k in range(SEGS):
            ynat_scr[k * seg_len:(k + 1) * seg_len, slab * 128:(slab + 1) * 128] = (
                y_scr[slab, pl.ds(k, seg_len, stride=SEGS), :])

    yy = (ynat_scr[...] + d_ref[...] * u.astype(F32)) * sz_ref[...].astype(F32)
    ys_ref[...] = jax.nn.gelu(yy).astype(BF16)
    for blk in range(N_BLK):
        sre_ref[0, :, blk * BLK_ST:(blk + 1) * BLK_ST] = carry_scr[blk, 0:1, :]
        sim_ref[0, :, blk * BLK_ST:(blk + 1) * BLK_ST] = carry_scr[blk, 1:2, :]


def _ssm_prompt(act, bblk, cblk, abar, apow, d_skip, *, tq):
    tiles = SEQ // tq
    n_half = BLK_ST // SCAN_W
    st = jax.ShapeDtypeStruct((BATCH, 1, N_GROUPS * P_STATE), F32)
    const3 = lambda b, i: (0, 0, 0)
    return pl.pallas_call(
        functools.partial(_ssm_prompt_kernel, tq=tq),
        grid=(BATCH, tiles),
        in_specs=[
            pl.BlockSpec((tq, E_SSM), lambda b, i: (b * tiles + i, 1)),
            pl.BlockSpec((tq, E_SSM), lambda b, i: (b * tiles + i, 2)),
            pl.BlockSpec((N_BLK, BLK_IN, 2 * BLK_ST), const3),
            pl.BlockSpec((N_BLK, 2 * BLK_ST, BLK_IN), const3),
            pl.BlockSpec((N_BLK, 2, BLK_ST), const3),
            pl.BlockSpec((N_BLK, 2, BLK_ST), const3),
            pl.BlockSpec((1, E_SSM), lambda b, i: (0, 0)),
        ],
        out_specs=[
            pl.BlockSpec((tq, E_SSM), lambda b, i: (b * tiles + i, 0)),
            pl.BlockSpec((1, 1, N_GROUPS * P_STATE), lambda b, i: (b, 0, 0)),
            pl.BlockSpec((1, 1, N_GROUPS * P_STATE), lambda b, i: (b, 0, 0)),
        ],
        out_shape=[jax.ShapeDtypeStruct((BATCH * SEQ, E_SSM), BF16), st, st],
        scratch_shapes=[
            pltpu.VMEM((2, tq, 2 * BLK_ST), F32),
            pltpu.VMEM((2, tq, 2 * BLK_ST), F32),
            pltpu.VMEM((E_SSM // 128, tq, 128), F32),
            pltpu.VMEM((tq, E_SSM), F32),
            pltpu.VMEM((N_BLK, 2, BLK_ST), F32),
            pltpu.VMEM((tq, tq), BF16),
        ],
        compiler_params=_cparams(("arbitrary", "arbitrary")),
        name="ssm_prompt",
    )(act, act, bblk, cblk, abar, apow, d_skip)


def _ssm_sample_kernel(u_ref, sz_ref, bblk_ref, cblk_ref, abar_ref, d_ref, s0r_ref, s0i_ref,
                       ys_ref, sre_ref, sim_ref, x_scr, s_scr):
    n_half = BLK_ST // SCAN_W
    rows = DEC_SEQ * DEC_BATCH
    u = u_ref[...]
    ys = []
    for blk in range(N_BLK):
        x = jnp.dot(u[:, blk * BLK_IN:(blk + 1) * BLK_IN], bblk_ref[blk],
                    preferred_element_type=F32)
        for cch in range(2 * n_half):
            x_scr[cch] = x[:, cch * SCAN_W:(cch + 1) * SCAN_W]
        for half in range(n_half):
            lo = half * SCAN_W
            col = blk * BLK_ST + lo
            ar = jnp.broadcast_to(abar_ref[blk, 0:1, lo:lo + SCAN_W], (8, SCAN_W))
            ai = jnp.broadcast_to(abar_ref[blk, 1:2, lo:lo + SCAN_W], (8, SCAN_W))

            def body(rc, carry, half=half, col=col, ar=ar, ai=ai):
                r0 = pl.multiple_of(rc * 8, 8)
                sr = s0r_ref[pl.ds(r0, 8), col:col + SCAN_W]
                si = s0i_ref[pl.ds(r0, 8), col:col + SCAN_W]
                for t in range(DEC_SEQ):
                    row = pl.multiple_of(t * DEC_BATCH + r0, 8)
                    xr = x_scr[half, pl.ds(row, 8), :]
                    xi = x_scr[n_half + half, pl.ds(row, 8), :]
                    sr, si = _cmul_add(ar, ai, sr, si, xr, xi)
                    s_scr[half, pl.ds(row, 8), :] = sr
                    s_scr[n_half + half, pl.ds(row, 8), :] = si
                sre_ref[pl.ds(r0, 8), col:col + SCAN_W] = sr
                sim_ref[pl.ds(r0, 8), col:col + SCAN_W] = si
                return carry

            lax.fori_loop(0, DEC_BATCH // 8, body, 0)

        y = jnp.zeros((rows, BLK_IN), F32)
        for cch in range(2 * n_half):
            y = y + jnp.dot(s_scr[cch].astype(BF16),
                            cblk_ref[blk, cch * SCAN_W:(cch + 1) * SCAN_W, :],
                            preferred_element_type=F32)
        ys.append(y)
    y_all = jnp.concatenate(ys, axis=1)
    yy = (y_all + d_ref[...] * u.astype(F32)) * sz_ref[...].astype(F32)
    ys_ref[...] = jax.nn.gelu(yy).astype(BF16)


def _ssm_sample(act, bblk, cblk, abar, d_skip, s0r, s0i):
    rows = DEC_SEQ * DEC_BATCH
    n_half = BLK_ST // SCAN_W
    st = jax.ShapeDtypeStruct((DEC_BATCH, N_GROUPS * P_STATE), F32)
    full = lambda shape: pl.BlockSpec(shape, lambda i: tuple(0 for _ in shape))
    return pl.pallas_call(
        _ssm_sample_kernel,
        grid=(1,),
        in_specs=[
            pl.BlockSpec((rows, E_SSM), lambda i: (0, 1)),
            pl.BlockSpec((rows, E_SSM), lambda i: (0, 2)),
            full((N_BLK, BLK_IN, 2 * BLK_ST)),
            full((N_BLK, 2 * BLK_ST, BLK_IN)),
            full((N_BLK, 2, BLK_ST)),
            full((1, E_SSM)),
            full((DEC_BATCH, N_GROUPS * P_STATE)),
            full((DEC_BATCH, N_GROUPS * P_STATE)),
        ],
        out_specs=[full((rows, E_SSM)), full((DEC_BATCH, N_GROUPS * P_STATE)),
                   full((DEC_BATCH, N_GROUPS * P_STATE))],
        out_shape=[jax.ShapeDtypeStruct((rows, E_SSM), BF16), st, st],
        scratch_shapes=[
            pltpu.VMEM((2 * n_half, rows, SCAN_W), F32),
            pltpu.VMEM((2 * n_half, rows, SCAN_W), F32),
        ],
        compiler_params=_cparams(("arbitrary",)),
        name="ssm_sample",
    )(act, act, bblk, cblk, abar, d_skip, s0r, s0i)


def _softmax_rows(s):
    m = jnp.max(s, axis=-1, keepdims=True)
    e = jnp.exp(s - m)
    return e / jnp.sum(e, axis=-1, keepdims=True)


def _attn_prompt_kernel(q_ref, az_ref, k_ref, v_ref, o_ref):
    for h in range(N_HEADS):
        cols = slice(h * HEAD_DIM, (h + 1) * HEAD_DIM)
        kh = k_ref[:, cols].astype(BF16)
        vh = v_ref[:, cols].astype(BF16)
        s = lax.dot_general(q_ref[:, cols], kh, (((1,), (1,)), ((), ())),
                            preferred_element_type=F32)
        p = _softmax_rows(s).astype(BF16)
        o = jnp.dot(p, vh, preferred_element_type=F32)
        o_ref[:, cols] = (o * az_ref[:, cols].astype(F32)).astype(BF16)


def _attn_prompt(act, mk2d, mv2d, *, tr):
    tiles = SEQ // tr
    return pl.pallas_call(
        _attn_prompt_kernel,
        grid=(BATCH, tiles),
        in_specs=[
            pl.BlockSpec((tr, E_ATTN), lambda b, i: (b * tiles + i, 3)),
            pl.BlockSpec((tr, E_ATTN), lambda b, i: (b * tiles + i, 4)),
            pl.BlockSpec((MEM_LEN, E_ATTN), lambda b, i: (b, 0)),
            pl.BlockSpec((MEM_LEN, E_ATTN), lambda b, i: (b, 0)),
        ],
        out_specs=pl.BlockSpec((tr, E_ATTN), lambda b, i: (b * tiles + i, 0)),
        out_shape=jax.ShapeDtypeStruct((BATCH * SEQ, E_ATTN), BF16),
        compiler_params=_cparams(("arbitrary", "arbitrary")),
        name="attn_prompt",
    )(act, act, mk2d, mv2d)


def _attn_sample_kernel(q_ref, az_ref, k_ref, v_ref, o_ref, *, bt):
    n_kv = bt * MEM_LEN * N_HEADS
    k = k_ref[...].reshape(n_kv, HEAD_DIM).astype(BF16)
    v = v_ref[...].reshape(n_kv, HEAD_DIM).astype(BF16)
    rows = DEC_SEQ * bt
    q = jnp.concatenate(
        [q_ref[:, :, h * HEAD_DIM:(h + 1) * HEAD_DIM].reshape(rows, HEAD_DIM)
         for h in range(N_HEADS)], axis=0)
    s = lax.dot_general(q, k, (((1,), (1,)), ((), ())), preferred_element_type=F32)
    row = lax.broadcasted_iota(jnp.int32, s.shape, 0)
    col = lax.broadcasted_iota(jnp.int32, s.shape, 1)
    same = ((row // rows == col % N_HEADS)
            & (row % bt == col // (MEM_LEN * N_HEADS)))
    s = jnp.where(same, s, -1e30)
    p = _softmax_rows(s).astype(BF16)
    o = jnp.dot(p, v, preferred_element_type=F32)
    for h in range(N_HEADS):
        cols = slice(h * HEAD_DIM, (h + 1) * HEAD_DIM)
        az = az_ref[:, :, cols].reshape(rows, HEAD_DIM).astype(F32)
        o_ref[:, :, cols] = (o[h * rows:(h + 1) * rows] * az).astype(BF16).reshape(
            DEC_SEQ, bt, HEAD_DIM)


def _attn_sample(act3, k4, v4, *, bt):
    kv_spec = pl.BlockSpec((bt, MEM_LEN, N_HEADS, HEAD_DIM), lambda b: (b, 0, 0, 0))
    return pl.pallas_call(
        functools.partial(_attn_sample_kernel, bt=bt),
        grid=(DEC_BATCH // bt,),
        in_specs=[
            pl.BlockSpec((DEC_SEQ, bt, E_ATTN), lambda b: (0, b, 3)),
            pl.BlockSpec((DEC_SEQ, bt, E_ATTN), lambda b: (0, b, 4)),
            kv_spec, kv_spec,
        ],
        out_specs=pl.BlockSpec((DEC_SEQ, bt, E_ATTN), lambda b: (0, b, 0)),
        out_shape=jax.ShapeDtypeStruct((DEC_SEQ, DEC_BATCH, E_ATTN), BF16),
        compiler_params=_cparams(("arbitrary",)),
        name="attn_sample",
    )(act3, act3, k4, v4)


def _merge_kernel(gated_ref, ys_ref, oz_ref, g0_ref, g1_ref, g2_ref,
                  wco_ref, wga_ref, wgb_ref, wao_ref, o_ref, *wbf_refs, emit_w):
    w_refs = (wco_ref, wga_ref, wgb_ref, wao_ref)
    if emit_w:
        @pl.when(pl.program_id(1) == 0)
        def _():
            for w_ref, wbf_ref in zip(w_refs, wbf_refs):
                wbf_ref[...] = w_ref[...].astype(BF16)

        w_refs = wbf_refs

    dot = functools.partial(jnp.dot, preferred_element_type=F32)
    conv_out = dot(gated_ref[...], w_refs[0][...])
    ys = ys_ref[...]
    ssm_out = dot(ys, w_refs[1][...]) * _sigmoid(dot(ys, w_refs[2][...]))
    attn_out = dot(oz_ref[...], w_refs[3][...])
    merged = (g0_ref[...].astype(F32) * conv_out + g1_ref[...].astype(F32) * ssm_out
              + g2_ref[...].astype(F32) * attn_out)
    o_ref[...] = merged.astype(BF16)


def _merge(act, ys, oz, wco, wga, wgb, wao, *, tm):
    rows = act.shape[0]
    tn = 1024
    nn = D_MODEL // tn
    emit_w = wco.dtype == F32
    wspec = pl.BlockSpec((E_CONV, tn), lambda n, i: (0, n),
                         pipeline_mode=pl.Buffered(1) if emit_w else None)
    o_spec = pl.BlockSpec((tm, tn), lambda n, i: (i, n))
    o_shape = jax.ShapeDtypeStruct((rows, D_MODEL), BF16)
    wbf_specs = [pl.BlockSpec((E_CONV, tn), lambda n, i: (0, n))] * 4 if emit_w else []
    wbf_shapes = [jax.ShapeDtypeStruct((E_CONV, D_MODEL), BF16)] * 4 if emit_w else []
    return pl.pallas_call(
        functools.partial(_merge_kernel, emit_w=emit_w),
        grid=(nn, rows // tm),
        in_specs=[
            pl.BlockSpec((tm, CB), lambda n, i: (i, 0)),
            pl.BlockSpec((tm, E_SSM), lambda n, i: (i, 0)),
            pl.BlockSpec((tm, E_ATTN), lambda n, i: (i, 0)),
            pl.BlockSpec((tm, tn), lambda n, i: (i, 5 + n)),
            pl.BlockSpec((tm, tn), lambda n, i: (i, 5 + nn + n)),
            pl.BlockSpec((tm, tn), lambda n, i: (i, 5 + 2 * nn + n)),
            wspec, wspec, wspec, wspec,
        ],
        out_specs=[o_spec] + wbf_specs,
        out_shape=[o_shape] + wbf_shapes,
        compiler_params=_cparams(("arbitrary", "arbitrary")),
        name="merge",
    )(act, ys, oz, act, act, act, wco, wga, wgb, wao)


def _out_kernel(x_ref, m_ref, w_ref, g_ref, o_ref, *rest, emit_w):
    if emit_w:
        (wbf_ref,) = rest

        @pl.when(pl.program_id(0) == 0)
        def _():
            wbf_ref[...] = w_ref[...].astype(BF16)

        w_ref = wbf_ref
    y = x_ref[...] + jnp.dot(m_ref[...], w_ref[...], preferred_element_type=F32)
    o_ref[...] = _rms(y, g_ref[...])


def _outproj(x2d, merged, w, g, *, tm):
    rows = x2d.shape[0]
    emit_w = w.dtype == F32
    w_spec = pl.BlockSpec((D_MODEL, D_MODEL), lambda i: (0, 0),
                          pipeline_mode=pl.Buffered(1) if emit_w else None)
    y_spec = pl.BlockSpec((tm, D_MODEL), lambda i: (i, 0))
    y_shape = jax.ShapeDtypeStruct((rows, D_MODEL), F32)
    return pl.pallas_call(
        functools.partial(_out_kernel, emit_w=emit_w),
        grid=(rows // tm,),
        in_specs=[
            pl.BlockSpec((tm, D_MODEL), lambda i: (i, 0)),
            pl.BlockSpec((tm, D_MODEL), lambda i: (i, 0)),
            w_spec,
            pl.BlockSpec((1, D_MODEL), lambda i: (0, 0)),
        ],
        out_specs=[y_spec, pl.BlockSpec((D_MODEL, D_MODEL), lambda i: (0, 0))] if emit_w else y_spec,
        out_shape=[y_shape, jax.ShapeDtypeStruct((D_MODEL, D_MODEL), BF16)] if emit_w else y_shape,
        compiler_params=_cparams(("arbitrary",)),
        name="outproj",
    )(x2d, merged, w, g)


def kernel(x_prompt, x_sample, mem_prompt, cache_mem_k, cache_mem_v, state_conv, state_ssm_re, state_ssm_im, norm_g, mem_norm_g, w_in, conv_w, w_conv_out, ssm_lambda_re, ssm_lambda_im, ssm_log_dt, ssm_b_re, ssm_b_im, ssm_c_re, ssm_c_im, ssm_d, w_glu_a, w_glu_b, w_mem_k, w_mem_v, w_attn_out, w_out, final_norm_g):
    l = 0
    tq = 512
    w_branch = (w_conv_out[l], w_glu_a[l], w_glu_b[l], w_attn_out[l])
    g_in = norm_g[l][None, :]
    g_fin = final_norm_g[None, :]
    d_skip = ssm_d[l][None, :]

    n_sq = int(math.log2(tq // SEGS))
    abar, apow, bbr_t, bbi_t = _discretise(ssm_lambda_re[l], ssm_lambda_im[l], ssm_log_dt[l],
                                           ssm_b_re[l], ssm_b_im[l], n_sq)
    bblk, cblk = _block_diag_weights(bbr_t, bbi_t, ssm_c_re[l], ssm_c_im[l])

    rows_s = DEC_SEQ * DEC_BATCH
    xs = jnp.transpose(x_sample, (1, 0, 2)).reshape(rows_s, D_MODEL)
    halo_s = jnp.transpose(state_conv[l], (1, 0, 2)).reshape(1, 2 * DEC_BATCH, E_CONV)
    act_s, nconv_s, w_in_bf = _inproj(xs, g_in, w_in[l], halo_s, conv_w[l], tm=rows_s,
                                      shift=DEC_BATCH, pad=2 * DEC_BATCH, tiles_per_seq=1)

    xp = x_prompt.reshape(BATCH * SEQ, D_MODEL)
    tm_p = 1024
    act_p, nconv_p = _inproj(xp, g_in, w_in_bf, jnp.zeros((BATCH, 2, E_CONV), F32), conv_w[l],
                             tm=tm_p, shift=1, pad=8, tiles_per_seq=SEQ // tm_p)
    mem2d = mem_prompt.reshape(BATCH * MEM_LEN, D_MODEL)
    g_mem = mem_norm_g[l][None, :]
    mk, mv = _memproj(mem2d, g_mem, w_mem_k[l], w_mem_v[l])
    ys_p, sre_p, sim_p = _ssm_prompt(act_p, bblk, cblk, abar, apow, d_skip, tq=tq)
    oz_p = _attn_prompt(act_p, mk, mv, tr=1024)
    merged_p, *w_branch_bf = _merge(act_p, ys_p, oz_p, *w_branch, tm=512)
    y_p, wo_bf = _outproj(xp, merged_p, w_out[l], g_fin, tm=512)

    s0r = state_ssm_re[l].reshape(DEC_BATCH, N_GROUPS * P_STATE)
    s0i = state_ssm_im[l].reshape(DEC_BATCH, N_GROUPS * P_STATE)
    ys_s, sre_s, sim_s = _ssm_sample(act_s, bblk, cblk, abar, d_skip, s0r, s0i)
    k3 = cache_mem_k[l]
    v3 = cache_mem_v[l]
    oz_s = _attn_sample(act_s.reshape(DEC_SEQ, DEC_BATCH, N_ACT_BLOCKS * CB), k3, v3, bt=8)
    (merged_s,) = _merge(act_s, ys_s, oz_s.reshape(rows_s, E_ATTN), *w_branch_bf, tm=512)
    y_s = _outproj(xs, merged_s, wo_bf, g_fin, tm=512)

    y_prompt = y_p.reshape(BATCH, SEQ, D_MODEL)
    y_sample = jnp.transpose(y_s.reshape(DEC_SEQ, DEC_BATCH, D_MODEL), (1, 0, 2))
    st_shape_p = (1, BATCH, N_GROUPS, P_STATE)
    st_shape_s = (1, DEC_BATCH, N_GROUPS, P_STATE)
    new_conv_s = jnp.transpose(nconv_s.reshape(2, DEC_BATCH, E_CONV), (1, 0, 2))
    return (y_prompt, y_sample,
            mk.reshape(1, BATCH, MEM_LEN, N_HEADS, HEAD_DIM),
            mv.reshape(1, BATCH, MEM_LEN, N_HEADS, HEAD_DIM),
            nconv_p[None],
            sre_p.reshape(st_shape_p), sim_p.reshape(st_shape_p),
            new_conv_s[None],
            sre_s.reshape(st_shape_s), sim_s.reshape(st_shape_s))
```

```python
import functools
import math

import jax
import jax.numpy as jnp
from jax import lax
from jax.experimental import pallas as pl
from jax.experimental.pallas import tpu as pltpu

D_MODEL = 2048
BATCH = 4
SEQ = 2048
DEC_BATCH = 128
DEC_SEQ = 4
E_CONV = 1024
CONV_W = 3
E_SSM = 1024
GROUP = 16
N_GROUPS = 64
P_STATE = 64
N_HEADS = 4
HEAD_DIM = 256
E_ATTN = 1024
MEM_LEN = 256
N_IN = 14336
EPS = 1e-6

F32 = jnp.float32
BF16 = jnp.bfloat16

CB = 1024
N_ACT_BLOCKS = 11
GROUPS_PER_BLK = 16
N_BLK = N_GROUPS // GROUPS_PER_BLK
BLK_IN = GROUPS_PER_BLK * GROUP
BLK_ST = GROUPS_PER_BLK * P_STATE
SCAN_W = 512
SEGS = 8
VMEM_LIMIT = 56 * 1024 * 1024


def _cparams(sem):
    return pltpu.CompilerParams(dimension_semantics=sem, vmem_limit_bytes=VMEM_LIMIT)


def _sigmoid(x):
    return 0.5 * jnp.tanh(0.5 * x) + 0.5


def _silu(x):
    return x * _sigmoid(x)


def _rms(x, g):
    ms = jnp.mean(x * x, axis=-1, keepdims=True)
    return x * lax.rsqrt(ms + EPS) * g


def _inproj_kernel(x_ref, g_ref, w_ref, halo_ref, cw_ref, act_ref, nconv_ref,
                   h_scr, cb_scr, vext_scr, *, tm, shift, pad, tiles_per_seq):
    i = pl.program_id(0)
    j = pl.program_id(1)

    @pl.when(j == 0)
    def _():
        h_scr[...] = _rms(x_ref[...], g_ref[...]).astype(BF16)

    def proj():
        return jnp.dot(h_scr[...], w_ref[...].astype(BF16), preferred_element_type=F32)

    @pl.when(j == 0)
    def _():
        cb_scr[...] = proj()

    @pl.when(j == 1)
    def _():
        vext_scr[pl.ds(pad, tm), :] = proj()

    @pl.when(j == 2)
    def _():
        vext_scr[pl.ds(pad, tm), :] = vext_scr[pl.ds(pad, tm), :] * proj()

        @pl.when(i % tiles_per_seq == 0)
        def _():
            vext_scr[pl.ds(pad - 2 * shift, 2 * shift), :] = halo_ref[0]

    @pl.when(j == 3)
    def _():
        def conv_rows(n):
            return (cw_ref[0:1, :] * vext_scr[pl.ds(pad - 2 * shift, n), :]
                    + cw_ref[1:2, :] * vext_scr[pl.ds(pad - shift, n), :]
                    + cw_ref[2:3, :] * vext_scr[pl.ds(pad, n), :])

        z = _silu(proj())
        if shift % 8 == 0:
            act_ref[...] = (cb_scr[...] * conv_rows(tm) * z).astype(BF16)
        else:
            v = vext_scr[pl.ds(pad, tm), :]
            conv = (cw_ref[0:1, :] * pltpu.roll(v, 2 * shift, axis=0)
                    + cw_ref[1:2, :] * pltpu.roll(v, shift, axis=0) + cw_ref[2:3, :] * v)
            act_ref[...] = (cb_scr[...] * conv * z).astype(BF16)
            head = 16
            act_ref[0:head, :] = (cb_scr[0:head, :] * conv_rows(head) * z[0:head]).astype(BF16)
        tail = vext_scr[pl.ds(pad + tm - 2 * shift, 2 * shift), :]
        nconv_ref[0] = tail
        vext_scr[pl.ds(pad - 2 * shift, 2 * shift), :] = tail

    @pl.when(j == 4)
    def _():
        act_ref[...] = proj().astype(BF16)

    @pl.when(j == 6)
    def _():
        act_ref[...] = (proj() * (HEAD_DIM ** -0.5)).astype(BF16)

    @pl.when((j == 5) | (j == 7))
    def _():
        act_ref[...] = _silu(proj()).astype(BF16)

    @pl.when(j >= 8)
    def _():
        act_ref[...] = _sigmoid(proj()).astype(BF16)


def _inproj(x2d, g, w, halo, conv_w, *, tm, shift, pad, tiles_per_seq):
    rows = x2d.shape[0]
    n_seq = rows // (tm * tiles_per_seq)
    kern = functools.partial(_inproj_kernel, tm=tm, shift=shift, pad=pad,
                             tiles_per_seq=tiles_per_seq)
    return pl.pallas_call(
        kern,
        grid=(rows // tm, N_IN // CB),
        in_specs=[
            pl.BlockSpec((tm, D_MODEL), lambda i, j: (i, 0)),
            pl.BlockSpec((1, D_MODEL), lambda i, j: (0, 0)),
            pl.BlockSpec((D_MODEL, CB), lambda i, j: (0, j)),
            pl.BlockSpec((1, 2 * shift, E_CONV), lambda i, j: (i // tiles_per_seq, 0, 0)),
            pl.BlockSpec((CONV_W, E_CONV), lambda i, j: (0, 0)),
        ],
        out_specs=[
            pl.BlockSpec((tm, CB), lambda i, j: (i, jnp.maximum(j - 3, 0))),
            pl.BlockSpec((1, 2 * shift, E_CONV), lambda i, j: (i // tiles_per_seq, 0, 0)),
        ],
        out_shape=[
            jax.ShapeDtypeStruct((rows, N_ACT_BLOCKS * CB), BF16),
            jax.ShapeDtypeStruct((n_seq, 2 * shift, E_CONV), F32),
        ],
        scratch_shapes=[
            pltpu.VMEM((tm, D_MODEL), BF16),
            pltpu.VMEM((tm, E_CONV), F32),
            pltpu.VMEM((pad + tm, E_CONV), F32),
        ],
        compiler_params=_cparams(("arbitrary", "arbitrary")),
        name="inproj",
    )(x2d, g, w, halo, conv_w)


def _memproj_kernel(x_ref, g_ref, wk_ref, wv_ref, k_ref, v_ref):
    h = _rms(x_ref[...], g_ref[...]).astype(BF16)
    k_ref[...] = jnp.dot(h, wk_ref[...].astype(BF16), preferred_element_type=F32)
    v_ref[...] = jnp.dot(h, wv_ref[...].astype(BF16), preferred_element_type=F32)


def _memproj(mem2d, g, wk, wv):
    rows = mem2d.shape[0]
    tm = 512
    w_spec = pl.BlockSpec((D_MODEL, E_ATTN), lambda i: (0, 0), pipeline_mode=pl.Buffered(1))
    o_spec = pl.BlockSpec((tm, E_ATTN), lambda i: (i, 0))
    o_shape = jax.ShapeDtypeStruct((rows, E_ATTN), F32)
    return pl.pallas_call(
        _memproj_kernel,
        grid=(rows // tm,),
        in_specs=[
            pl.BlockSpec((tm, D_MODEL), lambda i: (i, 0)),
            pl.BlockSpec((1, D_MODEL), lambda i: (0, 0)),
            w_spec, w_spec,
        ],
        out_specs=[o_spec, o_spec],
        out_shape=[o_shape, o_shape],
        compiler_params=_cparams(("arbitrary",)),
        name="memproj",
    )(mem2d, g, wk, wv)


def _disc_kernel(lr_ref, li_ref, ldt_ref, bre_ref, bim_ref,
                 ar_ref, ai_ref, pr_ref, pi_ref, bbr_ref, bbi_ref, *, n_sq):
    lr = lr_ref[...]
    li = li_ref[...]
    dt = jnp.exp(ldt_ref[...])
    mag = jnp.exp(lr * dt)
    ang = li * dt
    ar = mag * jnp.cos(ang)
    ai = mag * jnp.sin(ang)
    den = lr * lr + li * li
    fr = ((ar - 1.0) * lr + ai * li) / den
    fi = (ai * lr - (ar - 1.0) * li) / den
    bre = bre_ref[...]
    bim = bim_ref[...]
    bbr_ref[...] = fr * bre - fi * bim
    bbi_ref[...] = fr * bim + fi * bre
    ar_ref[...] = ar
    ai_ref[...] = ai
    pr, pi = ar, ai
    for _ in range(n_sq):
        pr, pi = pr * pr - pi * pi, 2.0 * pr * pi
    pr_ref[...] = pr
    pi_ref[...] = pi


def _discretise(lam_re, lam_im, log_dt, b_re, b_im, n_sq):
    rows = N_GROUPS * GROUP
    rep = lambda a: jnp.repeat(a, GROUP, axis=0)
    lr = rep(lam_re)
    li = rep(lam_im)
    ldt = rep(jnp.broadcast_to(log_dt[:, None], (N_GROUPS, P_STATE)))
    bre_t = jnp.transpose(b_re, (0, 2, 1)).reshape(rows, P_STATE)
    bim_t = jnp.transpose(b_im, (0, 2, 1)).reshape(rows, P_STATE)
    sds = jax.ShapeDtypeStruct((rows, P_STATE), F32)
    outs = pl.pallas_call(
        functools.partial(_disc_kernel, n_sq=n_sq),
        out_shape=[sds] * 6,
        name="s5_disc",
    )(lr, li, ldt, bre_t, bim_t)
    ar, ai, pr, pi, bbr_t, bbi_t = outs
    pick = lambda a: a.reshape(N_GROUPS, GROUP, P_STATE)[:, 0, :].reshape(N_BLK, 1, BLK_ST)
    abar = jnp.concatenate([pick(ar), pick(ai)], axis=1)
    apow = jnp.concatenate([pick(pr), pick(pi)], axis=1)
    return abar, apow, bbr_t, bbi_t


def _block_diag_weights(bbr_t, bbi_t, c_re, c_im):
    st = jnp.arange(BLK_ST)
    spread = (jnp.arange(P_STATE)[:, None] == st[None, :] % P_STATE).astype(F32)
    same_group = (jnp.arange(BLK_IN)[:, None] // GROUP == st[None, :] // P_STATE).astype(F32)
    hi = lax.Precision.HIGHEST

    def b_side(a):
        a = a.reshape(N_BLK, BLK_IN, P_STATE)
        return jnp.einsum('bkp,pn->bkn', a, spread, precision=hi) * same_group

    def c_side(a):
        a = a.reshape(N_BLK, BLK_IN, P_STATE)
        return jnp.einsum('pn,bkp->bnk', spread, a, precision=hi) * same_group.T

    bblk = jnp.concatenate([b_side(bbr_t), b_side(bbi_t)], axis=2).astype(BF16)
    cblk = jnp.concatenate([c_side(c_re), c_side(-c_im)], axis=1).astype(BF16)
    return bblk, cblk


def _cmul_add(ar, ai, sr, si, xr, xi):
    return ar * sr - ai * si + xr, ar * si + ai * sr + xi


def _ssm_prompt_kernel(u_ref, sz_ref, bblk_ref, cblk_ref, abar_ref, apow_ref, d_ref,
                       ys_ref, sre_ref, sim_ref,
                       x_scr, s_scr, y_scr, ynat_scr, carry_scr, perm_scr, *, tq):
    i = pl.program_id(1)
    seg_len = tq // SEGS
    n_half = BLK_ST // SCAN_W

    @pl.when(i == 0)
    def _():
        carry_scr[...] = jnp.zeros_like(carry_scr)

    @pl.when((pl.program_id(0) == 0) & (i == 0))
    def _():
        r = lax.broadcasted_iota(jnp.int32, (tq, tq), 0)
        c = lax.broadcasted_iota(jnp.int32, (tq, tq), 1)
        perm_scr[...] = jnp.where(c == (r % SEGS) * seg_len + r // SEGS, 1.0, 0.0).astype(BF16)

    u = u_ref[...]
    u_perm = jnp.dot(perm_scr[...], u, preferred_element_type=F32).astype(BF16)
    row_id = lax.broadcasted_iota(jnp.int32, (SEGS, SCAN_W), 0)

    for blk in range(N_BLK):
        par = blk % 2
        x_scr[par] = jnp.dot(u_perm[:, blk * BLK_IN:(blk + 1) * BLK_IN], bblk_ref[blk],
                             preferred_element_type=F32)
        for half in range(n_half):
            lo = half * SCAN_W
            re_cols = slice(lo, lo + SCAN_W)
            im_cols = slice(BLK_ST + lo, BLK_ST + lo + SCAN_W)
            ar = jnp.broadcast_to(abar_ref[blk, 0:1, lo:lo + SCAN_W], (SEGS, SCAN_W))
            ai = jnp.broadcast_to(abar_ref[blk, 1:2, lo:lo + SCAN_W], (SEGS, SCAN_W))

            def local_end(j, carry, par=par, re_cols=re_cols, im_cols=im_cols, ar=ar, ai=ai):
                row = pl.multiple_of(j * SEGS, SEGS)
                xr = x_scr[par, pl.ds(row, SEGS), re_cols]
                xi = x_scr[par, pl.ds(row, SEGS), im_cols]
                return _cmul_add(ar, ai, carry[0], carry[1], xr, xi)

            zero = jnp.zeros((SEGS, SCAN_W), F32)
            er, ei = lax.fori_loop(0, seg_len, local_end, (zero, zero), unroll=True)

            pr = apow_ref[blk, 0:1, lo:lo + SCAN_W]
            pi = apow_ref[blk, 1:2, lo:lo + SCAN_W]
            prev_r = carry_scr[blk, 0:1, lo:lo + SCAN_W]
            prev_i = carry_scr[blk, 1:2, lo:lo + SCAN_W]
            init_r = zero
            init_i = zero
            for k in range(SEGS):
                init_r = jnp.where(row_id == k, prev_r, init_r)
                init_i = jnp.where(row_id == k, prev_i, init_i)
                prev_r, prev_i = _cmul_add(pr, pi, prev_r, prev_i,
                                           er[k:k + 1, :], ei[k:k + 1, :])
            carry_scr[blk, 0:1, lo:lo + SCAN_W] = prev_r
            carry_scr[blk, 1:2, lo:lo + SCAN_W] = prev_i

            def full_scan(j, carry, par=par, re_cols=re_cols, im_cols=im_cols, ar=ar, ai=ai):
                row = pl.multiple_of(j * SEGS, SEGS)
                xr = x_scr[par, pl.ds(row, SEGS), re_cols]
                xi = x_scr[par, pl.ds(row, SEGS), im_cols]
                sr, si = _cmul_add(ar, ai, carry[0], carry[1], xr, xi)
                s_scr[par, pl.ds(row, SEGS), re_cols] = sr
                s_scr[par, pl.ds(row, SEGS), im_cols] = si
                return sr, si

            lax.fori_loop(0, seg_len, full_scan, (init_r, init_i), unroll=True)

        y = jnp.dot(s_scr[par].astype(BF16), cblk_ref[blk], preferred_element_type=F32)
        for sl in range(BLK_IN // 128):
            y_scr[blk * (BLK_IN // 128) + sl] = y[:, sl * 128:(sl + 1) * 128]

    for slab in range(E_SSM // 128):
        for k in range(SEGS):
            ynat_scr[k * seg_len:(k + 1) * seg_len, slab * 128:(slab + 1) * 128] = (
                y_scr[slab, pl.ds(k, seg_len, stride=SEGS), :])

    yy = (ynat_scr[...] + d_ref[...] * u.astype(F32)) * sz_ref[...].astype(F32)
    ys_ref[...] = jax.nn.gelu(yy).astype(BF16)
    for blk in range(N_BLK):
        sre_ref[0, :, blk * BLK_ST:(blk + 1) * BLK_ST] = carry_scr[blk, 0:1, :]
        sim_ref[0, :, blk * BLK_ST:(blk + 1) * BLK_ST] = carry_scr[blk, 1:2, :]


def _ssm_prompt(act, bblk, cblk, abar, apow, d_skip, *, tq):
    tiles = SEQ // tq
    n_half = BLK_ST // SCAN_W
    st = jax.ShapeDtypeStruct((BATCH, 1, N_GROUPS * P_STATE), F32)
    const3 = lambda b, i: (0, 0, 0)
    return pl.pallas_call(
        functools.partial(_ssm_prompt_kernel, tq=tq),
        grid=(BATCH, tiles),
        in_specs=[
            pl.BlockSpec((tq, E_SSM), lambda b, i: (b * tiles + i, 1)),
            pl.BlockSpec((tq, E_SSM), lambda b, i: (b * tiles + i, 2)),
            pl.BlockSpec((N_BLK, BLK_IN, 2 * BLK_ST), const3),
            pl.BlockSpec((N_BLK, 2 * BLK_ST, BLK_IN), const3),
            pl.BlockSpec((N_BLK, 2, BLK_ST), const3),
            pl.BlockSpec((N_BLK, 2, BLK_ST), const3),
            pl.BlockSpec((1, E_SSM), lambda b, i: (0, 0)),
        ],
        out_specs=[
            pl.BlockSpec((tq, E_SSM), lambda b, i: (b * tiles + i, 0)),
            pl.BlockSpec((1, 1, N_GROUPS * P_STATE), lambda b, i: (b, 0, 0)),
            pl.BlockSpec((1, 1, N_GROUPS * P_STATE), lambda b, i: (b, 0, 0)),
        ],
        out_shape=[jax.ShapeDtypeStruct((BATCH * SEQ, E_SSM), BF16), st, st],
        scratch_shapes=[
            pltpu.VMEM((2, tq, 2 * BLK_ST), F32),
            pltpu.VMEM((2, tq, 2 * BLK_ST), F32),
            pltpu.VMEM((E_SSM // 128, tq, 128), F32),
            pltpu.VMEM((tq, E_SSM), F32),
            pltpu.VMEM((N_BLK, 2, BLK_ST), F32),
            pltpu.VMEM((tq, tq), BF16),
        ],
        compiler_params=_cparams(("arbitrary", "arbitrary")),
        name="ssm_prompt",
    )(act, act, bblk, cblk, abar, apow, d_skip)


def _ssm_sample_kernel(u_ref, sz_ref, bblk_ref, cblk_ref, abar_ref, d_ref, s0r_ref, s0i_ref,
                       ys_ref, sre_ref, sim_ref, x_scr, s_scr):
    n_half = BLK_ST // SCAN_W
    rows = DEC_SEQ * DEC_BATCH
    u = u_ref[...]
    ys = []
    for blk in range(N_BLK):
        x = jnp.dot(u[:, blk * BLK_IN:(blk + 1) * BLK_IN], bblk_ref[blk],
                    preferred_element_type=F32)
        for cch in range(2 * n_half):
            x_scr[cch] = x[:, cch * SCAN_W:(cch + 1) * SCAN_W]
        for half in range(n_half):
            lo = half * SCAN_W
            col = blk * BLK_ST + lo
            ar = jnp.broadcast_to(abar_ref[blk, 0:1, lo:lo + SCAN_W], (8, SCAN_W))
            ai = jnp.broadcast_to(abar_ref[blk, 1:2, lo:lo + SCAN_W], (8, SCAN_W))

            def body(rc, carry, half=half, col=col, ar=ar, ai=ai):
                r0 = pl.multiple_of(rc * 8, 8)
                sr = s0r_ref[pl.ds(r0, 8), col:col + SCAN_W]
                si = s0i_ref[pl.ds(r0, 8), col:col + SCAN_W]
                for t in range(DEC_SEQ):
                    row = pl.multiple_of(t * DEC_BATCH + r0, 8)
                    xr = x_scr[half, pl.ds(row, 8), :]
                    xi = x_scr[n_half + half, pl.ds(row, 8), :]
                    sr, si = _cmul_add(ar, ai, sr, si, xr, xi)
                    s_scr[half, pl.ds(row, 8), :] = sr
                    s_scr[n_half + half, pl.ds(row, 8), :] = si
                sre_ref[pl.ds(r0, 8), col:col + SCAN_W] = sr
                sim_ref[pl.ds(r0, 8), col:col + SCAN_W] = si
                return carry

            lax.fori_loop(0, DEC_BATCH // 8, body, 0)

        y = jnp.zeros((rows, BLK_IN), F32)
        for cch in range(2 * n_half):
            y = y + jnp.dot(s_scr[cch].astype(BF16),
                            cblk_ref[blk, cch * SCAN_W:(cch + 1) * SCAN_W, :],
                            preferred_element_type=F32)
        ys.append(y)
    y_all = jnp.concatenate(ys, axis=1)
    yy = (y_all + d_ref[...] * u.astype(F32)) * sz_ref[...].astype(F32)
    ys_ref[...] = jax.nn.gelu(yy).astype(BF16)


def _ssm_sample(act, bblk, cblk, abar, d_skip, s0r, s0i):
    rows = DEC_SEQ * DEC_BATCH
    n_half = BLK_ST // SCAN_W
    st = jax.ShapeDtypeStruct((DEC_BATCH, N_GROUPS * P_STATE), F32)
    full = lambda shape: pl.BlockSpec(shape, lambda i: tuple(0 for _ in shape))
    return pl.pallas_call(
        _ssm_sample_kernel,
        grid=(1,),
        in_specs=[
            pl.BlockSpec((rows, E_SSM), lambda i: (0, 1)),
            pl.BlockSpec((rows, E_SSM), lambda i: (0, 2)),
            full((N_BLK, BLK_IN, 2 * BLK_ST)),
            full((N_BLK, 2 * BLK_ST, BLK_IN)),
            full((N_BLK, 2, BLK_ST)),
            full((1, E_SSM)),
            full((DEC_BATCH, N_GROUPS * P_STATE)),
            full((DEC_BATCH, N_GROUPS * P_STATE)),
        ],
        out_specs=[full((rows, E_SSM)), full((DEC_BATCH, N_GROUPS * P_STATE)),
                   full((DEC_BATCH, N_GROUPS * P_STATE))],
        out_shape=[jax.ShapeDtypeStruct((rows, E_SSM), BF16), st, st],
        scratch_shapes=[
            pltpu.VMEM((2 * n_half, rows, SCAN_W), F32),
            pltpu.VMEM((2 * n_half, rows, SCAN_W), F32),
        ],
        compiler_params=_cparams(("arbitrary",)),
        name="ssm_sample",
    )(act, act, bblk, cblk, abar, d_skip, s0r, s0i)


def _softmax_rows(s):
    m = jnp.max(s, axis=-1, keepdims=True)
    e = jnp.exp(s - m)
    return e / jnp.sum(e, axis=-1, keepdims=True)


def _attn_prompt_kernel(q_ref, az_ref, k_ref, v_ref, o_ref):
    for h in range(N_HEADS):
        cols = slice(h * HEAD_DIM, (h + 1) * HEAD_DIM)
        kh = k_ref[:, cols].astype(BF16)
        vh = v_ref[:, cols].astype(BF16)
        s = lax.dot_general(q_ref[:, cols], kh, (((1,), (1,)), ((), ())),
                            preferred_element_type=F32)
        p = _softmax_rows(s).astype(BF16)
        o = jnp.dot(p, vh, preferred_element_type=F32)
        o_ref[:, cols] = (o * az_ref[:, cols].astype(F32)).astype(BF16)


def _attn_prompt(act, mk2d, mv2d, *, tr):
    tiles = SEQ // tr
    return pl.pallas_call(
        _attn_prompt_kernel,
        grid=(BATCH, tiles),
        in_specs=[
            pl.BlockSpec((tr, E_ATTN), lambda b, i: (b * tiles + i, 3)),
            pl.BlockSpec((tr, E_ATTN), lambda b, i: (b * tiles + i, 4)),
            pl.BlockSpec((MEM_LEN, E_ATTN), lambda b, i: (b, 0)),
            pl.BlockSpec((MEM_LEN, E_ATTN), lambda b, i: (b, 0)),
        ],
        out_specs=pl.BlockSpec((tr, E_ATTN), lambda b, i: (b * tiles + i, 0)),
        out_shape=jax.ShapeDtypeStruct((BATCH * SEQ, E_ATTN), BF16),
        compiler_params=_cparams(("arbitrary", "arbitrary")),
        name="attn_prompt",
    )(act, act, mk2d, mv2d)


def _attn_sample_kernel(q_ref, az_ref, k_ref, v_ref, o_ref, *, bt):
    n_kv = bt * MEM_LEN * N_HEADS
    k = k_ref[...].reshape(n_kv, HEAD_DIM).astype(BF16)
    v = v_ref[...].reshape(n_kv, HEAD_DIM).astype(BF16)
    rows = DEC_SEQ * bt
    q = jnp.concatenate(
        [q_ref[:, :, h * HEAD_DIM:(h + 1) * HEAD_DIM].reshape(rows, HEAD_DIM)
         for h in range(N_HEADS)], axis=0)
    s = lax.dot_general(q, k, (((1,), (1,)), ((), ())), preferred_element_type=F32)
    row = lax.broadcasted_iota(jnp.int32, s.shape, 0)
    col = lax.broadcasted_iota(jnp.int32, s.shape, 1)
    same = ((row // rows == col % N_HEADS)
            & (row % bt == col // (MEM_LEN * N_HEADS)))
    s = jnp.where(same, s, -1e30)
    p = _softmax_rows(s).astype(BF16)
    o = jnp.dot(p, v, preferred_element_type=F32)
    for h in range(N_HEADS):
        cols = slice(h * HEAD_DIM, (h + 1) * HEAD_DIM)
        az = az_ref[:, :, cols].reshape(rows, HEAD_DIM).astype(F32)
        o_ref[:, :, cols] = (o[h * rows:(h + 1) * rows] * az).astype(BF16).reshape(
            DEC_SEQ, bt, HEAD_DIM)


def _attn_sample(act3, k4, v4, *, bt):
    kv_spec = pl.BlockSpec((bt, MEM_LEN, N_HEADS, HEAD_DIM), lambda b: (b, 0, 0, 0))
    return pl.pallas_call(
        functools.partial(_attn_sample_kernel, bt=bt),
        grid=(DEC_BATCH // bt,),
        in_specs=[
            pl.BlockSpec((DEC_SEQ, bt, E_ATTN), lambda b: (0, b, 3)),
            pl.BlockSpec((DEC_SEQ, bt, E_ATTN), lambda b: (0, b, 4)),
            kv_spec, kv_spec,
        ],
        out_specs=pl.BlockSpec((DEC_SEQ, bt, E_ATTN), lambda b: (0, b, 0)),
        out_shape=jax.ShapeDtypeStruct((DEC_SEQ, DEC_BATCH, E_ATTN), BF16),
        compiler_params=_cparams(("arbitrary",)),
        name="attn_sample",
    )(act3, act3, k4, v4)


def _merge_kernel(gated_ref, ys_ref, oz_ref, g0_ref, g1_ref, g2_ref,
                  wco_ref, wga_ref, wgb_ref, wao_ref, o_ref, *wbf_refs, emit_w):
    w_refs = (wco_ref, wga_ref, wgb_ref, wao_ref)
    if emit_w:
        @pl.when(pl.program_id(1) == 0)
        def _():
            for w_ref, wbf_ref in zip(w_refs, wbf_refs):
                wbf_ref[...] = w_ref[...].astype(BF16)

        w_refs = wbf_refs

    dot = functools.partial(jnp.dot, preferred_element_type=F32)
    conv_out = dot(gated_ref[...], w_refs[0][...])
    ys = ys_ref[...]
    ssm_out = dot(ys, w_refs[1][...]) * _sigmoid(dot(ys, w_refs[2][...]))
    attn_out = dot(oz_ref[...], w_refs[3][...])
    merged = (g0_ref[...].astype(F32) * conv_out + g1_ref[...].astype(F32) * ssm_out
              + g2_ref[...].astype(F32) * attn_out)
    o_ref[...] = merged.astype(BF16)


def _merge(act, ys, oz, wco, wga, wgb, wao, *, tm):
    rows = act.shape[0]
    tn = 1024
    nn = D_MODEL // tn
    emit_w = wco.dtype == F32
    wspec = pl.BlockSpec((E_CONV, tn), lambda n, i: (0, n),
                         pipeline_mode=pl.Buffered(1) if emit_w else None)
    o_spec = pl.BlockSpec((tm, tn), lambda n, i: (i, n))
    o_shape = jax.ShapeDtypeStruct((rows, D_MODEL), BF16)
    wbf_specs = [pl.BlockSpec((E_CONV, tn), lambda n, i: (0, n))] * 4 if emit_w else []
    wbf_shapes = [jax.ShapeDtypeStruct((E_CONV, D_MODEL), BF16)] * 4 if emit_w else []
    return pl.pallas_call(
        functools.partial(_merge_kernel, emit_w=emit_w),
        grid=(nn, rows // tm),
        in_specs=[
            pl.BlockSpec((tm, CB), lambda n, i: (i, 0)),
            pl.BlockSpec((tm, E_SSM), lambda n, i: (i, 0)),
            pl.BlockSpec((tm, E_ATTN), lambda n, i: (i, 0)),
            pl.BlockSpec((tm, tn), lambda n, i: (i, 5 + n)),
            pl.BlockSpec((tm, tn), lambda n, i: (i, 5 + nn + n)),
            pl.BlockSpec((tm, tn), lambda n, i: (i, 5 + 2 * nn + n)),
            wspec, wspec, wspec, wspec,
        ],
        out_specs=[o_spec] + wbf_specs,
        out_shape=[o_shape] + wbf_shapes,
        compiler_params=_cparams(("arbitrary", "arbitrary")),
        name="merge",
    )(act, ys, oz, act, act, act, wco, wga, wgb, wao)


def _out_kernel(x_ref, m_ref, w_ref, g_ref, o_ref, *rest, emit_w):
    if emit_w:
        (wbf_ref,) = rest

        @pl.when(pl.program_id(0) == 0)
        def _():
            wbf_ref[...] = w_ref[...].astype(BF16)

        w_ref = wbf_ref
    y = x_ref[...] + jnp.dot(m_ref[...], w_ref[...], preferred_element_type=F32)
    o_ref[...] = _rms(y, g_ref[...])


def _outproj(x2d, merged, w, g, *, tm):
    rows = x2d.shape[0]
    emit_w = w.dtype == F32
    w_spec = pl.BlockSpec((D_MODEL, D_MODEL), lambda i: (0, 0),
                          pipeline_mode=pl.Buffered(1) if emit_w else None)
    y_spec = pl.BlockSpec((tm, D_MODEL), lambda i: (i, 0))
    y_shape = jax.ShapeDtypeStruct((rows, D_MODEL), F32)
    return pl.pallas_call(
        functools.partial(_out_kernel, emit_w=emit_w),
        grid=(rows // tm,),
        in_specs=[
            pl.BlockSpec((tm, D_MODEL), lambda i: (i, 0)),
            pl.BlockSpec((tm, D_MODEL), lambda i: (i, 0)),
            w_spec,
            pl.BlockSpec((1, D_MODEL), lambda i: (0, 0)),
        ],
        out_specs=[y_spec, pl.BlockSpec((D_MODEL, D_MODEL), lambda i: (0, 0))] if emit_w else y_spec,
        out_shape=[y_shape, jax.ShapeDtypeStruct((D_MODEL, D_MODEL), BF16)] if emit_w else y_shape,
        compiler_params=_cparams(("arbitrary",)),
        name="outproj",
    )(x2d, merged, w, g)


def kernel(x_prompt, x_sample, mem_prompt, cache_mem_k, cache_mem_v, state_conv, state_ssm_re, state_ssm_im, norm_g, mem_norm_g, w_in, conv_w, w_conv_out, ssm_lambda_re, ssm_lambda_im, ssm_log_dt, ssm_b_re, ssm_b_im, ssm_c_re, ssm_c_im, ssm_d, w_glu_a, w_glu_b, w_mem_k, w_mem_v, w_attn_out, w_out, final_norm_g):
    l = 0
    tq = 512
    w_branch = (w_conv_out[l], w_glu_a[l], w_glu_b[l], w_attn_out[l])
    g_in = norm_g[l][None, :]
    g_fin = final_norm_g[None, :]
    d_skip = ssm_d[l][None, :]

    n_sq = int(math.log2(tq // SEGS))
    abar, apow, bbr_t, bbi_t = _discretise(ssm_lambda_re[l], ssm_lambda_im[l], ssm_log_dt[l],
                                           ssm_b_re[l], ssm_b_im[l], n_sq)
    bblk, cblk = _block_diag_weights(bbr_t, bbi_t, ssm_c_re[l], ssm_c_im[l])

    rows_s = DEC_SEQ * DEC_BATCH
    xs = jnp.transpose(x_sample, (1, 0, 2)).reshape(rows_s, D_MODEL)
    halo_s = jnp.transpose(state_conv[l], (1, 0, 2)).reshape(1, 2 * DEC_BATCH, E_CONV)
    act_s, nconv_s = _inproj(xs, g_in, w_in[l], halo_s, conv_w[l], tm=rows_s,
                             shift=DEC_BATCH, pad=2 * DEC_BATCH, tiles_per_seq=1)

    xp = x_prompt.reshape(BATCH * SEQ, D_MODEL)
    tm_p = 1024
    act_p, nconv_p = _inproj(xp, g_in, w_in[l], jnp.zeros((BATCH, 2, E_CONV), F32), conv_w[l],
                             tm=tm_p, shift=1, pad=8, tiles_per_seq=SEQ // tm_p)
    mem2d = mem_prompt.reshape(BATCH * MEM_LEN, D_MODEL)
    g_mem = mem_norm_g[l][None, :]
    mk, mv = _memproj(mem2d, g_mem, w_mem_k[l], w_mem_v[l])
    ys_p, sre_p, sim_p = _ssm_prompt(act_p, bblk, cblk, abar, apow, d_skip, tq=tq)
    oz_p = _attn_prompt(act_p, mk, mv, tr=1024)
    merged_p, *w_branch_bf = _merge(act_p, ys_p, oz_p, *w_branch, tm=512)
    y_p, wo_bf = _outproj(xp, merged_p, w_out[l], g_fin, tm=512)

    s0r = state_ssm_re[l].reshape(DEC_BATCH, N_GROUPS * P_STATE)
    s0i = state_ssm_im[l].reshape(DEC_BATCH, N_GROUPS * P_STATE)
    ys_s, sre_s, sim_s = _ssm_sample(act_s, bblk, cblk, abar, d_skip, s0r, s0i)
    k3 = cache_mem_k[l]
    v3 = cache_mem_v[l]
    oz_s = _attn_sample(act_s.reshape(DEC_SEQ, DEC_BATCH, N_ACT_BLOCKS * CB), k3, v3, bt=8)
    (merged_s,) = _merge(act_s, ys_s, oz_s.reshape(rows_s, E_ATTN), *w_branch_bf, tm=512)
    y_s = _outproj(xs, merged_s, wo_bf, g_fin, tm=512)

    y_prompt = y_p.reshape(BATCH, SEQ, D_MODEL)
    y_sample = jnp.transpose(y_s.reshape(DEC_SEQ, DEC_BATCH, D_MODEL), (1, 0, 2))
    st_shape_p = (1, BATCH, N_GROUPS, P_STATE)
    st_shape_s = (1, DEC_BATCH, N_GROUPS, P_STATE)
    new_conv_s = jnp.transpose(nconv_s.reshape(2, DEC_BATCH, E_CONV), (1, 0, 2))
    return (y_prompt, y_sample,
            mk.reshape(1, BATCH, MEM_LEN, N_HEADS, HEAD_DIM),
            mv.reshape(1, BATCH, MEM_LEN, N_HEADS, HEAD_DIM),
            nconv_p[None],
            sre_p.reshape(st_shape_p), sim_p.reshape(st_shape_p),
            new_conv_s[None],
            sre_s.reshape(st_shape_s), sim_s.reshape(st_shape_s))
```

```python
import functools
import math

import jax
import jax.numpy as jnp
from jax import lax
from jax.experimental import pallas as pl
from jax.experimental.pallas import tpu as pltpu

D_MODEL = 2048
BATCH = 4
SEQ = 2048
DEC_BATCH = 128
DEC_SEQ = 4
E_CONV = 1024
CONV_W = 3
E_SSM = 1024
GROUP = 16
N_GROUPS = 64
P_STATE = 64
N_HEADS = 4
HEAD_DIM = 256
E_ATTN = 1024
MEM_LEN = 256
N_IN = 14336
EPS = 1e-6

F32 = jnp.float32
BF16 = jnp.bfloat16

CB = 1024
N_ACT_BLOCKS = 11
GROUPS_PER_BLK = 16
N_BLK = N_GROUPS // GROUPS_PER_BLK
BLK_IN = GROUPS_PER_BLK * GROUP
BLK_ST = GROUPS_PER_BLK * P_STATE
SCAN_W = 512
SEGS = 8
SSM_BUFS = 2

TM_INPROJ = 1024
TQ_SSM = 512
TR_ATTN = 1024
TM_TAIL = 512
BT_ATTN = 8
VMEM_LIMIT = 56 * 1024 * 1024


def _cparams(sem):
    return pltpu.CompilerParams(dimension_semantics=sem, vmem_limit_bytes=VMEM_LIMIT)


def _sigmoid(x):
    return 0.5 * jnp.tanh(0.5 * x) + 0.5


def _silu(x):
    return x * _sigmoid(x)


def _rms(x, g):
    ms = jnp.mean(x * x, axis=-1, keepdims=True)
    return x * lax.rsqrt(ms + EPS) * g


def _inproj_kernel(x_ref, g_ref, w_ref, halo_ref, cw_ref, act_ref, nconv_ref,
                   h_scr, cb_scr, vext_scr, *, tm, shift, pad, tiles_per_seq):
    i = pl.program_id(0)
    j = pl.program_id(1)

    @pl.when(j == 0)
    def _():
        h_scr[...] = _rms(x_ref[...], g_ref[...]).astype(BF16)

    def proj():
        return jnp.dot(h_scr[...], w_ref[...].astype(BF16), preferred_element_type=F32)

    @pl.when(j == 0)
    def _():
        cb_scr[...] = proj()

    @pl.when(j == 1)
    def _():
        vext_scr[pl.ds(pad, tm), :] = proj()

    @pl.when(j == 2)
    def _():
        vext_scr[pl.ds(pad, tm), :] = vext_scr[pl.ds(pad, tm), :] * proj()

        @pl.when(i % tiles_per_seq == 0)
        def _():
            vext_scr[pl.ds(pad - 2 * shift, 2 * shift), :] = halo_ref[0]

    @pl.when(j == 3)
    def _():
        def conv_rows(n):
            return (cw_ref[0:1, :] * vext_scr[pl.ds(pad - 2 * shift, n), :]
                    + cw_ref[1:2, :] * vext_scr[pl.ds(pad - shift, n), :]
                    + cw_ref[2:3, :] * vext_scr[pl.ds(pad, n), :])

        z = _silu(proj())
        if shift % 8 == 0:
            act_ref[...] = (cb_scr[...] * conv_rows(tm) * z).astype(BF16)
        else:
            v = vext_scr[pl.ds(pad, tm), :]
            conv = (cw_ref[0:1, :] * pltpu.roll(v, 2 * shift, axis=0)
                    + cw_ref[1:2, :] * pltpu.roll(v, shift, axis=0) + cw_ref[2:3, :] * v)
            act_ref[...] = (cb_scr[...] * conv * z).astype(BF16)
            head = 16
            act_ref[0:head, :] = (cb_scr[0:head, :] * conv_rows(head) * z[0:head]).astype(BF16)
        tail = vext_scr[pl.ds(pad + tm - 2 * shift, 2 * shift), :]
        nconv_ref[0] = tail
        vext_scr[pl.ds(pad - 2 * shift, 2 * shift), :] = tail

    @pl.when(j == 4)
    def _():
        act_ref[...] = proj().astype(BF16)

    @pl.when(j == 6)
    def _():
        act_ref[...] = (proj() * (HEAD_DIM ** -0.5)).astype(BF16)

    @pl.when((j == 5) | (j == 7))
    def _():
        act_ref[...] = _silu(proj()).astype(BF16)

    @pl.when(j >= 8)
    def _():
        act_ref[...] = _sigmoid(proj()).astype(BF16)


def _inproj(x2d, g, w, halo, conv_w, *, tm, shift, pad, tiles_per_seq):
    rows = x2d.shape[0]
    n_seq = rows // (tm * tiles_per_seq)
    kern = functools.partial(_inproj_kernel, tm=tm, shift=shift, pad=pad,
                             tiles_per_seq=tiles_per_seq)
    return pl.pallas_call(
        kern,
        grid=(rows // tm, N_IN // CB),
        in_specs=[
            pl.BlockSpec((tm, D_MODEL), lambda i, j: (i, 0)),
            pl.BlockSpec((1, D_MODEL), lambda i, j: (0, 0)),
            pl.BlockSpec((D_MODEL, CB), lambda i, j: (0, j)),
            pl.BlockSpec((1, 2 * shift, E_CONV), lambda i, j: (i // tiles_per_seq, 0, 0)),
            pl.BlockSpec((CONV_W, E_CONV), lambda i, j: (0, 0)),
        ],
        out_specs=[
            pl.BlockSpec((tm, CB), lambda i, j: (i, jnp.maximum(j - 3, 0))),
            pl.BlockSpec((1, 2 * shift, E_CONV), lambda i, j: (i // tiles_per_seq, 0, 0)),
        ],
        out_shape=[
            jax.ShapeDtypeStruct((rows, N_ACT_BLOCKS * CB), BF16),
            jax.ShapeDtypeStruct((n_seq, 2 * shift, E_CONV), F32),
        ],
        scratch_shapes=[
            pltpu.VMEM((tm, D_MODEL), BF16),
            pltpu.VMEM((tm, E_CONV), F32),
            pltpu.VMEM((pad + tm, E_CONV), F32),
        ],
        compiler_params=_cparams(("arbitrary", "arbitrary")),
        name="inproj",
    )(x2d, g, w, halo, conv_w)


def _memproj_kernel(x_ref, g_ref, wk_ref, wv_ref, k_ref, v_ref):
    h = _rms(x_ref[...], g_ref[...]).astype(BF16)
    k_ref[...] = jnp.dot(h, wk_ref[...].astype(BF16), preferred_element_type=F32)
    v_ref[...] = jnp.dot(h, wv_ref[...].astype(BF16), preferred_element_type=F32)


def _memproj(mem2d, g, wk, wv):
    rows = mem2d.shape[0]
    tm = 512
    w_spec = pl.BlockSpec((D_MODEL, E_ATTN), lambda i: (0, 0), pipeline_mode=pl.Buffered(1))
    o_spec = pl.BlockSpec((tm, E_ATTN), lambda i: (i, 0))
    o_shape = jax.ShapeDtypeStruct((rows, E_ATTN), F32)
    return pl.pallas_call(
        _memproj_kernel,
        grid=(rows // tm,),
        in_specs=[
            pl.BlockSpec((tm, D_MODEL), lambda i: (i, 0)),
            pl.BlockSpec((1, D_MODEL), lambda i: (0, 0)),
            w_spec, w_spec,
        ],
        out_specs=[o_spec, o_spec],
        out_shape=[o_shape, o_shape],
        compiler_params=_cparams(("arbitrary",)),
        name="memproj",
    )(mem2d, g, wk, wv)


def _disc_kernel(lr_ref, li_ref, ldt_ref, bre_ref, bim_ref,
                 ar_ref, ai_ref, pr_ref, pi_ref, bbr_ref, bbi_ref, *, n_sq):
    lr = lr_ref[...]
    li = li_ref[...]
    dt = jnp.exp(ldt_ref[...])
    mag = jnp.exp(lr * dt)
    ang = li * dt
    ar = mag * jnp.cos(ang)
    ai = mag * jnp.sin(ang)
    den = lr * lr + li * li
    fr = ((ar - 1.0) * lr + ai * li) / den
    fi = (ai * lr - (ar - 1.0) * li) / den
    bre = bre_ref[...]
    bim = bim_ref[...]
    bbr_ref[...] = fr * bre - fi * bim
    bbi_ref[...] = fr * bim + fi * bre
    ar_ref[...] = ar
    ai_ref[...] = ai
    pr, pi = ar, ai
    for _ in range(n_sq):
        pr, pi = pr * pr - pi * pi, 2.0 * pr * pi
    pr_ref[...] = pr
    pi_ref[...] = pi


def _discretise(lam_re, lam_im, log_dt, b_re, b_im, n_sq):
    rows = N_GROUPS * GROUP
    rep = lambda a: jnp.repeat(a, GROUP, axis=0)
    lr = rep(lam_re)
    li = rep(lam_im)
    ldt = rep(jnp.broadcast_to(log_dt[:, None], (N_GROUPS, P_STATE)))
    bre_t = jnp.transpose(b_re, (0, 2, 1)).reshape(rows, P_STATE)
    bim_t = jnp.transpose(b_im, (0, 2, 1)).reshape(rows, P_STATE)
    sds = jax.ShapeDtypeStruct((rows, P_STATE), F32)
    outs = pl.pallas_call(
        functools.partial(_disc_kernel, n_sq=n_sq),
        out_shape=[sds] * 6,
        name="s5_disc",
    )(lr, li, ldt, bre_t, bim_t)
    ar, ai, pr, pi, bbr_t, bbi_t = outs
    pick = lambda a: a.reshape(N_GROUPS, GROUP, P_STATE)[:, 0, :].reshape(N_BLK, 1, BLK_ST)
    abar = jnp.concatenate([pick(ar), pick(ai)], axis=1)
    apow = jnp.concatenate([pick(pr), pick(pi)], axis=1)
    return abar, apow, bbr_t, bbi_t


def _block_diag_weights(bbr_t, bbi_t, c_re, c_im):
    st = jnp.arange(BLK_ST)
    spread = (jnp.arange(P_STATE)[:, None] == st[None, :] % P_STATE).astype(F32)
    same_group = (jnp.arange(BLK_IN)[:, None] // GROUP == st[None, :] // P_STATE).astype(F32)
    hi = lax.Precision.HIGHEST

    def b_side(a):
        a = a.reshape(N_BLK, BLK_IN, P_STATE)
        return jnp.einsum('bkp,pn->bkn', a, spread, precision=hi) * same_group

    def c_side(a):
        a = a.reshape(N_BLK, BLK_IN, P_STATE)
        return jnp.einsum('pn,bkp->bnk', spread, a, precision=hi) * same_group.T

    bblk = jnp.concatenate([b_side(bbr_t), b_side(bbi_t)], axis=2).astype(BF16)
    cblk = jnp.concatenate([c_side(c_re), c_side(-c_im)], axis=1).astype(BF16)
    return bblk, cblk


def _cmul_add(ar, ai, sr, si, xr, xi):
    return ar * sr - ai * si + xr, ar * si + ai * sr + xi


def _ssm_prompt_kernel(u_ref, sz_ref, bblk_ref, cblk_ref, abar_ref, apow_ref, d_ref,
                       ys_ref, sre_ref, sim_ref,
                       x_scr, s_scr, y_scr, ynat_scr, carry_scr, perm_scr, *, tq):
    i = pl.program_id(1)
    seg_len = tq // SEGS
    n_half = BLK_ST // SCAN_W

    @pl.when(i == 0)
    def _():
        carry_scr[...] = jnp.zeros_like(carry_scr)

    @pl.when((pl.program_id(0) == 0) & (i == 0))
    def _():
        r = lax.broadcasted_iota(jnp.int32, (tq, tq), 0)
        c = lax.broadcasted_iota(jnp.int32, (tq, tq), 1)
        perm_scr[...] = jnp.where(c == (r % SEGS) * seg_len + r // SEGS, 1.0, 0.0).astype(BF16)

    u = u_ref[...]
    u_perm = jnp.dot(perm_scr[...], u, preferred_element_type=F32).astype(BF16)
    row_id = lax.broadcasted_iota(jnp.int32, (SEGS, SCAN_W), 0)

    for blk in range(N_BLK):
        par = blk % SSM_BUFS
        x_scr[par] = jnp.dot(u_perm[:, blk * BLK_IN:(blk + 1) * BLK_IN], bblk_ref[blk],
                             preferred_element_type=F32)
        for half in range(n_half):
            lo = half * SCAN_W
            re_cols = slice(lo, lo + SCAN_W)
            im_cols = slice(BLK_ST + lo, BLK_ST + lo + SCAN_W)
            ar = jnp.broadcast_to(abar_ref[blk, 0:1, lo:lo + SCAN_W], (SEGS, SCAN_W))
            ai = jnp.broadcast_to(abar_ref[blk, 1:2, lo:lo + SCAN_W], (SEGS, SCAN_W))

            def local_end(j, carry, par=par, re_cols=re_cols, im_cols=im_cols, ar=ar, ai=ai):
                row = pl.multiple_of(j * SEGS, SEGS)
                xr = x_scr[par, pl.ds(row, SEGS), re_cols]
                xi = x_scr[par, pl.ds(row, SEGS), im_cols]
                return _cmul_add(ar, ai, carry[0], carry[1], xr, xi)

            zero = jnp.zeros((SEGS, SCAN_W), F32)
            er, ei = lax.fori_loop(0, seg_len, local_end, (zero, zero), unroll=True)

            pr = apow_ref[blk, 0:1, lo:lo + SCAN_W]
            pi = apow_ref[blk, 1:2, lo:lo + SCAN_W]
            prev_r = carry_scr[blk, 0:1, lo:lo + SCAN_W]
            prev_i = carry_scr[blk, 1:2, lo:lo + SCAN_W]
            init_r = zero
            init_i = zero
            for k in range(SEGS):
                init_r = jnp.where(row_id == k, prev_r, init_r)
                init_i = jnp.where(row_id == k, prev_i, init_i)
                prev_r, prev_i = _cmul_add(pr, pi, prev_r, prev_i,
                                           er[k:k + 1, :], ei[k:k + 1, :])
            carry_scr[blk, 0:1, lo:lo + SCAN_W] = prev_r
            carry_scr[blk, 1:2, lo:lo + SCAN_W] = prev_i

            def full_scan(j, carry, par=par, re_cols=re_cols, im_cols=im_cols, ar=ar, ai=ai):
                row = pl.multiple_of(j * SEGS, SEGS)
                xr = x_scr[par, pl.ds(row, SEGS), re_cols]
                xi = x_scr[par, pl.ds(row, SEGS), im_cols]
                sr, si = _cmul_add(ar, ai, carry[0], carry[1], xr, xi)
                s_scr[par, pl.ds(row, SEGS), re_cols] = sr
                s_scr[par, pl.ds(row, SEGS), im_cols] = si
                return sr, si

            lax.fori_loop(0, seg_len, full_scan, (init_r, init_i), unroll=True)

        y = jnp.dot(s_scr[par].astype(BF16), cblk_ref[blk], preferred_element_type=F32)
        for sl in range(BLK_IN // 128):
            y_scr[blk * (BLK_IN // 128) + sl] = y[:, sl * 128:(sl + 1) * 128]

    for slab in range(E_SSM // 128):
        for k in range(SEGS):
            ynat_scr[k * seg_len:(k + 1) * seg_len, slab * 128:(slab + 1) * 128] = (
                y_scr[slab, pl.ds(k, seg_len, stride=SEGS), :])

    yy = (ynat_scr[...] + d_ref[...] * u.astype(F32)) * sz_ref[...].astype(F32)
    ys_ref[...] = jax.nn.gelu(yy).astype(BF16)
    for blk in range(N_BLK):
        sre_ref[0, :, blk * BLK_ST:(blk + 1) * BLK_ST] = carry_scr[blk, 0:1, :]
        sim_ref[0, :, blk * BLK_ST:(blk + 1) * BLK_ST] = carry_scr[blk, 1:2, :]


def _ssm_prompt(act, bblk, cblk, abar, apow, d_skip, *, tq):
    tiles = SEQ // tq
    st = jax.ShapeDtypeStruct((BATCH, 1, N_GROUPS * P_STATE), F32)
    const3 = lambda b, i: (0, 0, 0)
    return pl.pallas_call(
        functools.partial(_ssm_prompt_kernel, tq=tq),
        grid=(BATCH, tiles),
        in_specs=[
            pl.BlockSpec((tq, E_SSM), lambda b, i: (b * tiles + i, 1)),
            pl.BlockSpec((tq, E_SSM), lambda b, i: (b * tiles + i, 2)),
            pl.BlockSpec((N_BLK, BLK_IN, 2 * BLK_ST), const3),
            pl.BlockSpec((N_BLK, 2 * BLK_ST, BLK_IN), const3),
            pl.BlockSpec((N_BLK, 2, BLK_ST), const3),
            pl.BlockSpec((N_BLK, 2, BLK_ST), const3),
            pl.BlockSpec((1, E_SSM), lambda b, i: (0, 0)),
        ],
        out_specs=[
            pl.BlockSpec((tq, E_SSM), lambda b, i: (b * tiles + i, 0)),
            pl.BlockSpec((1, 1, N_GROUPS * P_STATE), lambda b, i: (b, 0, 0)),
            pl.BlockSpec((1, 1, N_GROUPS * P_STATE), lambda b, i: (b, 0, 0)),
        ],
        out_shape=[jax.ShapeDtypeStruct((BATCH * SEQ, E_SSM), BF16), st, st],
        scratch_shapes=[
            pltpu.VMEM((SSM_BUFS, tq, 2 * BLK_ST), F32),
            pltpu.VMEM((SSM_BUFS, tq, 2 * BLK_ST), F32),
            pltpu.VMEM((E_SSM // 128, tq, 128), F32),
            pltpu.VMEM((tq, E_SSM), F32),
            pltpu.VMEM((N_BLK, 2, BLK_ST), F32),
            pltpu.VMEM((tq, tq), BF16),
        ],
        compiler_params=_cparams(("arbitrary", "arbitrary")),
        name="ssm_prompt",
    )(act, act, bblk, cblk, abar, apow, d_skip)


def _ssm_sample_kernel(u_ref, sz_ref, bblk_ref, cblk_ref, abar_ref, d_ref, s0r_ref, s0i_ref,
                       ys_ref, sre_ref, sim_ref, x_scr, s_scr):
    n_half = BLK_ST // SCAN_W
    rows = DEC_SEQ * DEC_BATCH
    u = u_ref[...]
    ys = []
    for blk in range(N_BLK):
        x = jnp.dot(u[:, blk * BLK_IN:(blk + 1) * BLK_IN], bblk_ref[blk],
                    preferred_element_type=F32)
        for cch in range(2 * n_half):
            x_scr[cch] = x[:, cch * SCAN_W:(cch + 1) * SCAN_W]
        for half in range(n_half):
            lo = half * SCAN_W
            col = blk * BLK_ST + lo
            ar = jnp.broadcast_to(abar_ref[blk, 0:1, lo:lo + SCAN_W], (8, SCAN_W))
            ai = jnp.broadcast_to(abar_ref[blk, 1:2, lo:lo + SCAN_W], (8, SCAN_W))

            def body(rc, carry, half=half, col=col, ar=ar, ai=ai):
                r0 = pl.multiple_of(rc * 8, 8)
                sr = s0r_ref[pl.ds(r0, 8), col:col + SCAN_W]
                si = s0i_ref[pl.ds(r0, 8), col:col + SCAN_W]
                for t in range(DEC_SEQ):
                    row = pl.multiple_of(t * DEC_BATCH + r0, 8)
                    xr = x_scr[half, pl.ds(row, 8), :]
                    xi = x_scr[n_half + half, pl.ds(row, 8), :]
                    sr, si = _cmul_add(ar, ai, sr, si, xr, xi)
                    s_scr[half, pl.ds(row, 8), :] = sr
                    s_scr[n_half + half, pl.ds(row, 8), :] = si
                sre_ref[pl.ds(r0, 8), col:col + SCAN_W] = sr
                sim_ref[pl.ds(r0, 8), col:col + SCAN_W] = si
                return carry

            lax.fori_loop(0, DEC_BATCH // 8, body, 0)

        y = jnp.zeros((rows, BLK_IN), F32)
        for cch in range(2 * n_half):
            y = y + jnp.dot(s_scr[cch].astype(BF16),
                            cblk_ref[blk, cch * SCAN_W:(cch + 1) * SCAN_W, :],
                            preferred_element_type=F32)
        ys.append(y)
    y_all = jnp.concatenate(ys, axis=1)
    yy = (y_all + d_ref[...] * u.astype(F32)) * sz_ref[...].astype(F32)
    ys_ref[...] = jax.nn.gelu(yy).astype(BF16)


def _ssm_sample(act, bblk, cblk, abar, d_skip, s0r, s0i):
    rows = DEC_SEQ * DEC_BATCH
    n_half = BLK_ST // SCAN_W
    st = jax.ShapeDtypeStruct((DEC_BATCH, N_GROUPS * P_STATE), F32)
    full = lambda shape: pl.BlockSpec(shape, lambda i: tuple(0 for _ in shape))
    return pl.pallas_call(
        _ssm_sample_kernel,
        grid=(1,),
        in_specs=[
            pl.BlockSpec((rows, E_SSM), lambda i: (0, 1)),
            pl.BlockSpec((rows, E_SSM), lambda i: (0, 2)),
            full((N_BLK, BLK_IN, 2 * BLK_ST)),
            full((N_BLK, 2 * BLK_ST, BLK_IN)),
            full((N_BLK, 2, BLK_ST)),
            full((1, E_SSM)),
            full((DEC_BATCH, N_GROUPS * P_STATE)),
            full((DEC_BATCH, N_GROUPS * P_STATE)),
        ],
        out_specs=[full((rows, E_SSM)), full((DEC_BATCH, N_GROUPS * P_STATE)),
                   full((DEC_BATCH, N_GROUPS * P_STATE))],
        out_shape=[jax.ShapeDtypeStruct((rows, E_SSM), BF16), st, st],
        scratch_shapes=[
            pltpu.VMEM((2 * n_half, rows, SCAN_W), F32),
            pltpu.VMEM((2 * n_half, rows, SCAN_W), F32),
        ],
        compiler_params=_cparams(("arbitrary",)),
        name="ssm_sample",
    )(act, act, bblk, cblk, abar, d_skip, s0r, s0i)


def _softmax_rows(s):
    m = jnp.max(s, axis=-1, keepdims=True)
    e = jnp.exp(s - m)
    return e / jnp.sum(e, axis=-1, keepdims=True)


def _attn_prompt_kernel(q_ref, az_ref, k_ref, v_ref, o_ref):
    for h in range(N_HEADS):
        cols = slice(h * HEAD_DIM, (h + 1) * HEAD_DIM)
        kh = k_ref[:, cols].astype(BF16)
        vh = v_ref[:, cols].astype(BF16)
        s = lax.dot_general(q_ref[:, cols], kh, (((1,), (1,)), ((), ())),
                            preferred_element_type=F32)
        p = _softmax_rows(s).astype(BF16)
        o = jnp.dot(p, vh, preferred_element_type=F32)
        o_ref[:, cols] = (o * az_ref[:, cols].astype(F32)).astype(BF16)


def _attn_prompt(act, mk2d, mv2d, *, tr):
    tiles = SEQ // tr
    return pl.pallas_call(
        _attn_prompt_kernel,
        grid=(BATCH, tiles),
        in_specs=[
            pl.BlockSpec((tr, E_ATTN), lambda b, i: (b * tiles + i, 3)),
            pl.BlockSpec((tr, E_ATTN), lambda b, i: (b * tiles + i, 4)),
            pl.BlockSpec((MEM_LEN, E_ATTN), lambda b, i: (b, 0)),
            pl.BlockSpec((MEM_LEN, E_ATTN), lambda b, i: (b, 0)),
        ],
        out_specs=pl.BlockSpec((tr, E_ATTN), lambda b, i: (b * tiles + i, 0)),
        out_shape=jax.ShapeDtypeStruct((BATCH * SEQ, E_ATTN), BF16),
        compiler_params=_cparams(("arbitrary", "arbitrary")),
        name="attn_prompt",
    )(act, act, mk2d, mv2d)


def _attn_sample_kernel(q_ref, az_ref, k_ref, v_ref, o_ref, *, bt):
    n_kv = bt * MEM_LEN * N_HEADS
    k = k_ref[...].reshape(n_kv, HEAD_DIM).astype(BF16)
    v = v_ref[...].reshape(n_kv, HEAD_DIM).astype(BF16)
    rows = DEC_SEQ * bt
    q = jnp.concatenate(
        [q_ref[:, :, h * HEAD_DIM:(h + 1) * HEAD_DIM].reshape(rows, HEAD_DIM)
         for h in range(N_HEADS)], axis=0)
    s = lax.dot_general(q, k, (((1,), (1,)), ((), ())), preferred_element_type=F32)
    row = lax.broadcasted_iota(jnp.int32, s.shape, 0)
    col = lax.broadcasted_iota(jnp.int32, s.shape, 1)
    same = ((row // rows == col % N_HEADS)
            & (row % bt == col // (MEM_LEN * N_HEADS)))
    s = jnp.where(same, s, -1e30)
    p = _softmax_rows(s).astype(BF16)
    o = jnp.dot(p, v, preferred_element_type=F32)
    for h in range(N_HEADS):
        cols = slice(h * HEAD_DIM, (h + 1) * HEAD_DIM)
        az = az_ref[:, :, cols].reshape(rows, HEAD_DIM).astype(F32)
        o_ref[:, :, cols] = (o[h * rows:(h + 1) * rows] * az).astype(BF16).reshape(
            DEC_SEQ, bt, HEAD_DIM)


def _attn_sample(act3, k4, v4, *, bt):
    kv_spec = pl.BlockSpec((bt, MEM_LEN, N_HEADS, HEAD_DIM), lambda b: (b, 0, 0, 0))
    return pl.pallas_call(
        functools.partial(_attn_sample_kernel, bt=bt),
        grid=(DEC_BATCH // bt,),
        in_specs=[
            pl.BlockSpec((DEC_SEQ, bt, E_ATTN), lambda b: (0, b, 3)),
            pl.BlockSpec((DEC_SEQ, bt, E_ATTN), lambda b: (0, b, 4)),
            kv_spec, kv_spec,
        ],
        out_specs=pl.BlockSpec((DEC_SEQ, bt, E_ATTN), lambda b: (0, b, 0)),
        out_shape=jax.ShapeDtypeStruct((DEC_SEQ, DEC_BATCH, E_ATTN), BF16),
        compiler_params=_cparams(("arbitrary",)),
        name="attn_sample",
    )(act3, act3, k4, v4)


def _merge_kernel(gated_ref, ys_ref, oz_ref, g0_ref, g1_ref, g2_ref,
                  wco_ref, wga_ref, wgb_ref, wao_ref, o_ref, *wbf_refs, emit_w):
    w_refs = (wco_ref, wga_ref, wgb_ref, wao_ref)
    if emit_w:
        @pl.when(pl.program_id(1) == 0)
        def _():
            for w_ref, wbf_ref in zip(w_refs, wbf_refs):
                wbf_ref[...] = w_ref[...].astype(BF16)

        w_refs = wbf_refs

    dot = functools.partial(jnp.dot, preferred_element_type=F32)
    conv_out = dot(gated_ref[...], w_refs[0][...])
    ys = ys_ref[...]
    ssm_out = dot(ys, w_refs[1][...]) * _sigmoid(dot(ys, w_refs[2][...]))
    attn_out = dot(oz_ref[...], w_refs[3][...])
    merged = (g0_ref[...].astype(F32) * conv_out + g1_ref[...].astype(F32) * ssm_out
              + g2_ref[...].astype(F32) * attn_out)
    o_ref[...] = merged.astype(BF16)


def _merge(act, ys, oz, wco, wga, wgb, wao, *, tm):
    rows = act.shape[0]
    tn = 1024
    nn = D_MODEL // tn
    emit_w = wco.dtype == F32
    wspec = pl.BlockSpec((E_CONV, tn), lambda n, i: (0, n),
                         pipeline_mode=pl.Buffered(1) if emit_w else None)
    o_spec = pl.BlockSpec((tm, tn), lambda n, i: (i, n))
    o_shape = jax.ShapeDtypeStruct((rows, D_MODEL), BF16)
    wbf_specs = [pl.BlockSpec((E_CONV, tn), lambda n, i: (0, n))] * 4 if emit_w else []
    wbf_shapes = [jax.ShapeDtypeStruct((E_CONV, D_MODEL), BF16)] * 4 if emit_w else []
    return pl.pallas_call(
        functools.partial(_merge_kernel, emit_w=emit_w),
        grid=(nn, rows // tm),
        in_specs=[
            pl.BlockSpec((tm, CB), lambda n, i: (i, 0)),
            pl.BlockSpec((tm, E_SSM), lambda n, i: (i, 0)),
            pl.BlockSpec((tm, E_ATTN), lambda n, i: (i, 0)),
            pl.BlockSpec((tm, tn), lambda n, i: (i, 5 + n)),
            pl.BlockSpec((tm, tn), lambda n, i: (i, 5 + nn + n)),
            pl.BlockSpec((tm, tn), lambda n, i: (i, 5 + 2 * nn + n)),
            wspec, wspec, wspec, wspec,
        ],
        out_specs=[o_spec] + wbf_specs,
        out_shape=[o_shape] + wbf_shapes,
        compiler_params=_cparams(("arbitrary", "arbitrary")),
        name="merge",
    )(act, ys, oz, act, act, act, wco, wga, wgb, wao)


def _out_kernel(x_ref, m_ref, w_ref, g_ref, o_ref, *rest, emit_w):
    if emit_w:
        (wbf_ref,) = rest

        @pl.when(pl.program_id(0) == 0)
        def _():
            wbf_ref[...] = w_ref[...].astype(BF16)

        w_ref = wbf_ref
    y = x_ref[...] + jnp.dot(m_ref[...], w_ref[...], preferred_element_type=F32)
    o_ref[...] = _rms(y, g_ref[...])


def _outproj(x2d, merged, w, g, *, tm):
    rows = x2d.shape[0]
    emit_w = w.dtype == F32
    w_spec = pl.BlockSpec((D_MODEL, D_MODEL), lambda i: (0, 0),
                          pipeline_mode=pl.Buffered(1) if emit_w else None)
    y_spec = pl.BlockSpec((tm, D_MODEL), lambda i: (i, 0))
    y_shape = jax.ShapeDtypeStruct((rows, D_MODEL), F32)
    return pl.pallas_call(
        functools.partial(_out_kernel, emit_w=emit_w),
        grid=(rows // tm,),
        in_specs=[
            pl.BlockSpec((tm, D_MODEL), lambda i: (i, 0)),
            pl.BlockSpec((tm, D_MODEL), lambda i: (i, 0)),
            w_spec,
            pl.BlockSpec((1, D_MODEL), lambda i: (0, 0)),
        ],
        out_specs=[y_spec, pl.BlockSpec((D_MODEL, D_MODEL), lambda i: (0, 0))] if emit_w else y_spec,
        out_shape=[y_shape, jax.ShapeDtypeStruct((D_MODEL, D_MODEL), BF16)] if emit_w else y_shape,
        compiler_params=_cparams(("arbitrary",)),
        name="outproj",
    )(x2d, merged, w, g)


def kernel(x_prompt, x_sample, mem_prompt, cache_mem_k, cache_mem_v, state_conv, state_ssm_re, state_ssm_im, norm_g, mem_norm_g, w_in, conv_w, w_conv_out, ssm_lambda_re, ssm_lambda_im, ssm_log_dt, ssm_b_re, ssm_b_im, ssm_c_re, ssm_c_im, ssm_d, w_glu_a, w_glu_b, w_mem_k, w_mem_v, w_attn_out, w_out, final_norm_g):
    l = 0
    tq = TQ_SSM
    w_branch = (w_conv_out[l], w_glu_a[l], w_glu_b[l], w_attn_out[l])
    g_in = norm_g[l][None, :]
    g_fin = final_norm_g[None, :]
    d_skip = ssm_d[l][None, :]

    n_sq = int(math.log2(tq // SEGS))
    abar, apow, bbr_t, bbi_t = _discretise(ssm_lambda_re[l], ssm_lambda_im[l], ssm_log_dt[l],
                                           ssm_b_re[l], ssm_b_im[l], n_sq)
    bblk, cblk = _block_diag_weights(bbr_t, bbi_t, ssm_c_re[l], ssm_c_im[l])

    rows_s = DEC_SEQ * DEC_BATCH
    xs = jnp.transpose(x_sample, (1, 0, 2)).reshape(rows_s, D_MODEL)
    halo_s = jnp.transpose(state_conv[l], (1, 0, 2)).reshape(1, 2 * DEC_BATCH, E_CONV)
    act_s, nconv_s = _inproj(xs, g_in, w_in[l], halo_s, conv_w[l], tm=rows_s,
                             shift=DEC_BATCH, pad=2 * DEC_BATCH, tiles_per_seq=1)

    xp = x_prompt.reshape(BATCH * SEQ, D_MODEL)
    tm_p = TM_INPROJ
    act_p, nconv_p = _inproj(xp, g_in, w_in[l], jnp.zeros((BATCH, 2, E_CONV), F32), conv_w[l],
                             tm=tm_p, shift=1, pad=8, tiles_per_seq=SEQ // tm_p)
    mem2d = mem_prompt.reshape(BATCH * MEM_LEN, D_MODEL)
    g_mem = mem_norm_g[l][None, :]
    mk, mv = _memproj(mem2d, g_mem, w_mem_k[l], w_mem_v[l])
    ys_p, sre_p, sim_p = _ssm_prompt(act_p, bblk, cblk, abar, apow, d_skip, tq=tq)
    oz_p = _attn_prompt(act_p, mk, mv, tr=TR_ATTN)
    merged_p, *w_branch_bf = _merge(act_p, ys_p, oz_p, *w_branch, tm=TM_TAIL)
    y_p, wo_bf = _outproj(xp, merged_p, w_out[l], g_fin, tm=TM_TAIL)

    s0r = state_ssm_re[l].reshape(DEC_BATCH, N_GROUPS * P_STATE)
    s0i = state_ssm_im[l].reshape(DEC_BATCH, N_GROUPS * P_STATE)
    ys_s, sre_s, sim_s = _ssm_sample(act_s, bblk, cblk, abar, d_skip, s0r, s0i)
    oz_s = _attn_sample(act_s.reshape(DEC_SEQ, DEC_BATCH, N_ACT_BLOCKS * CB),
                        cache_mem_k[l], cache_mem_v[l], bt=BT_ATTN)
    (merged_s,) = _merge(act_s, ys_s, oz_s.reshape(rows_s, E_ATTN), *w_branch_bf, tm=TM_TAIL)
    y_s = _outproj(xs, merged_s, wo_bf, g_fin, tm=TM_TAIL)

    y_prompt = y_p.reshape(BATCH, SEQ, D_MODEL)
    y_sample = jnp.transpose(y_s.reshape(DEC_SEQ, DEC_BATCH, D_MODEL), (1, 0, 2))
    st_shape_p = (1, BATCH, N_GROUPS, P_STATE)
    st_shape_s = (1, DEC_BATCH, N_GROUPS, P_STATE)
    new_conv_s = jnp.transpose(nconv_s.reshape(2, DEC_BATCH, E_CONV), (1, 0, 2))
    return (y_prompt, y_sample,
            mk.reshape(1, BATCH, MEM_LEN, N_HEADS, HEAD_DIM),
            mv.reshape(1, BATCH, MEM_LEN, N_HEADS, HEAD_DIM),
            nconv_p[None],
            sre_p.reshape(st_shape_p), sim_p.reshape(st_shape_p),
            new_conv_s[None],
            sre_s.reshape(st_shape_s), sim_s.reshape(st_shape_s))
```

```python
import functools
import math

import jax
import jax.numpy as jnp
from jax import lax
from jax.experimental import pallas as pl
from jax.experimental.pallas import tpu as pltpu

D_MODEL = 2048
BATCH = 4
SEQ = 2048
DEC_BATCH = 128
DEC_SEQ = 4
E_CONV = 1024
CONV_W = 3
E_SSM = 1024
GROUP = 16
N_GROUPS = 64
P_STATE = 64
N_HEADS = 4
HEAD_DIM = 256
E_ATTN = 1024
MEM_LEN = 256
N_IN = 14336
EPS = 1e-6

F32 = jnp.float32
BF16 = jnp.bfloat16

CB = 1024
N_ACT_BLOCKS = 11
GROUPS_PER_BLK = 16
N_BLK = N_GROUPS // GROUPS_PER_BLK
BLK_IN = GROUPS_PER_BLK * GROUP
BLK_ST = GROUPS_PER_BLK * P_STATE
SCAN_W = 512
SEGS = 8
SSM_BUFS = 2
N_WBUF = 4

TM_INPROJ = 1024
TQ_SSM = 512
TR_ATTN = 1024
TM_TAIL = 512
BT_ATTN = 8
VMEM_LIMIT = 56 * 1024 * 1024


def _cparams(sem):
    return pltpu.CompilerParams(dimension_semantics=sem, vmem_limit_bytes=VMEM_LIMIT)


def _sigmoid(x):
    return 0.5 * jnp.tanh(0.5 * x) + 0.5


def _silu(x):
    return x * _sigmoid(x)


def _rms(x, g):
    ms = jnp.mean(x * x, axis=-1, keepdims=True)
    return x * lax.rsqrt(ms + EPS) * g


def _inproj_kernel(x_ref, g_ref, w_ref, halo_ref, cw_ref, act_ref, nconv_ref,
                   h_scr, cb_scr, vext_scr, *w_stream, tm, shift, pad, tiles_per_seq):
    i = pl.program_id(0)
    j = pl.program_id(1)
    n_col = N_IN // CB

    if w_stream:
        wbuf, wsem = w_stream

        def w_copy(blk):
            col = blk * CB if isinstance(blk, int) else pl.multiple_of(blk * CB, CB)
            slot = blk % N_WBUF
            return pltpu.make_async_copy(w_ref.at[:, pl.ds(col, CB)], wbuf.at[slot],
                                         wsem.at[slot])

        @pl.when(j == 0)
        def _():
            for blk in range(N_WBUF):
                w_copy(blk).start()

    @pl.when(j == 0)
    def _():
        h_scr[...] = _rms(x_ref[...], g_ref[...]).astype(BF16)

    if w_stream:
        w_copy(j).wait()

    def proj():
        w = wbuf[j % N_WBUF] if w_stream else w_ref[...]
        return jnp.dot(h_scr[...], w.astype(BF16), preferred_element_type=F32)

    @pl.when(j == 0)
    def _():
        cb_scr[...] = proj()

    @pl.when(j == 1)
    def _():
        vext_scr[pl.ds(pad, tm), :] = proj()

    @pl.when(j == 2)
    def _():
        vext_scr[pl.ds(pad, tm), :] = vext_scr[pl.ds(pad, tm), :] * proj()

        @pl.when(i % tiles_per_seq == 0)
        def _():
            vext_scr[pl.ds(pad - 2 * shift, 2 * shift), :] = halo_ref[0]

    @pl.when(j == 3)
    def _():
        def conv_rows(n):
            return (cw_ref[0:1, :] * vext_scr[pl.ds(pad - 2 * shift, n), :]
                    + cw_ref[1:2, :] * vext_scr[pl.ds(pad - shift, n), :]
                    + cw_ref[2:3, :] * vext_scr[pl.ds(pad, n), :])

        z = _silu(proj())
        if shift % 8 == 0:
            act_ref[...] = (cb_scr[...] * conv_rows(tm) * z).astype(BF16)
        else:
            v = vext_scr[pl.ds(pad, tm), :]
            conv = (cw_ref[0:1, :] * pltpu.roll(v, 2 * shift, axis=0)
                    + cw_ref[1:2, :] * pltpu.roll(v, shift, axis=0) + cw_ref[2:3, :] * v)
            act_ref[...] = (cb_scr[...] * conv * z).astype(BF16)
            head = 16
            act_ref[0:head, :] = (cb_scr[0:head, :] * conv_rows(head) * z[0:head]).astype(BF16)
        tail = vext_scr[pl.ds(pad + tm - 2 * shift, 2 * shift), :]
        nconv_ref[0] = tail
        vext_scr[pl.ds(pad - 2 * shift, 2 * shift), :] = tail

    @pl.when(j == 4)
    def _():
        act_ref[...] = proj().astype(BF16)

    @pl.when(j == 6)
    def _():
        act_ref[...] = (proj() * (HEAD_DIM ** -0.5)).astype(BF16)

    @pl.when((j == 5) | (j == 7))
    def _():
        act_ref[...] = _silu(proj()).astype(BF16)

    @pl.when(j >= 8)
    def _():
        act_ref[...] = _sigmoid(proj()).astype(BF16)

    if w_stream:
        @pl.when(j + N_WBUF < n_col)
        def _():
            w_copy(j + N_WBUF).start()


def _inproj(x2d, g, w, halo, conv_w, *, tm, shift, pad, tiles_per_seq):
    rows = x2d.shape[0]
    n_seq = rows // (tm * tiles_per_seq)
    stream_w = rows == tm
    kern = functools.partial(_inproj_kernel, tm=tm, shift=shift, pad=pad,
                             tiles_per_seq=tiles_per_seq)
    w_spec = (pl.BlockSpec(memory_space=pl.ANY) if stream_w
              else pl.BlockSpec((D_MODEL, CB), lambda i, j: (0, j)))
    w_scratch = ([pltpu.VMEM((N_WBUF, D_MODEL, CB), F32), pltpu.SemaphoreType.DMA((N_WBUF,))]
                 if stream_w else [])
    return pl.pallas_call(
        kern,
        grid=(rows // tm, N_IN // CB),
        in_specs=[
            pl.BlockSpec((tm, D_MODEL), lambda i, j: (i, 0)),
            pl.BlockSpec((1, D_MODEL), lambda i, j: (0, 0)),
            w_spec,
            pl.BlockSpec((1, 2 * shift, E_CONV), lambda i, j: (i // tiles_per_seq, 0, 0)),
            pl.BlockSpec((CONV_W, E_CONV), lambda i, j: (0, 0)),
        ],
        out_specs=[
            pl.BlockSpec((tm, CB), lambda i, j: (i, jnp.maximum(j - 3, 0))),
            pl.BlockSpec((1, 2 * shift, E_CONV), lambda i, j: (i // tiles_per_seq, 0, 0)),
        ],
        out_shape=[
            jax.ShapeDtypeStruct((rows, N_ACT_BLOCKS * CB), BF16),
            jax.ShapeDtypeStruct((n_seq, 2 * shift, E_CONV), F32),
        ],
        scratch_shapes=[
            pltpu.VMEM((tm, D_MODEL), BF16),
            pltpu.VMEM((tm, E_CONV), F32),
            pltpu.VMEM((pad + tm, E_CONV), F32),
        ] + w_scratch,
        compiler_params=_cparams(("arbitrary", "arbitrary")),
        name="inproj",
    )(x2d, g, w, halo, conv_w)


def _memproj_kernel(x_ref, g_ref, wk_ref, wv_ref, k_ref, v_ref):
    h = _rms(x_ref[...], g_ref[...]).astype(BF16)
    k_ref[...] = jnp.dot(h, wk_ref[...].astype(BF16), preferred_element_type=F32)
    v_ref[...] = jnp.dot(h, wv_ref[...].astype(BF16), preferred_element_type=F32)


def _memproj(mem2d, g, wk, wv):
    rows = mem2d.shape[0]
    tm = 512
    w_spec = pl.BlockSpec((D_MODEL, E_ATTN), lambda i: (0, 0), pipeline_mode=pl.Buffered(1))
    o_spec = pl.BlockSpec((tm, E_ATTN), lambda i: (i, 0))
    o_shape = jax.ShapeDtypeStruct((rows, E_ATTN), F32)
    return pl.pallas_call(
        _memproj_kernel,
        grid=(rows // tm,),
        in_specs=[
            pl.BlockSpec((tm, D_MODEL), lambda i: (i, 0)),
            pl.BlockSpec((1, D_MODEL), lambda i: (0, 0)),
            w_spec, w_spec,
        ],
        out_specs=[o_spec, o_spec],
        out_shape=[o_shape, o_shape],
        compiler_params=_cparams(("arbitrary",)),
        name="memproj",
    )(mem2d, g, wk, wv)


def _disc_kernel(lr_ref, li_ref, ldt_ref, bre_ref, bim_ref,
                 ar_ref, ai_ref, pr_ref, pi_ref, bbr_ref, bbi_ref, *, n_sq):
    lr = lr_ref[...]
    li = li_ref[...]
    dt = jnp.exp(ldt_ref[...])
    mag = jnp.exp(lr * dt)
    ang = li * dt
    ar = mag * jnp.cos(ang)
    ai = mag * jnp.sin(ang)
    den = lr * lr + li * li
    fr = ((ar - 1.0) * lr + ai * li) / den
    fi = (ai * lr - (ar - 1.0) * li) / den
    bre = bre_ref[...]
    bim = bim_ref[...]
    bbr_ref[...] = fr * bre - fi * bim
    bbi_ref[...] = fr * bim + fi * bre
    ar_ref[...] = ar
    ai_ref[...] = ai
    pr, pi = ar, ai
    for _ in range(n_sq):
        pr, pi = pr * pr - pi * pi, 2.0 * pr * pi
    pr_ref[...] = pr
    pi_ref[...] = pi


def _discretise(lam_re, lam_im, log_dt, b_re, b_im, n_sq):
    rows = N_GROUPS * GROUP
    rep = lambda a: jnp.repeat(a, GROUP, axis=0)
    lr = rep(lam_re)
    li = rep(lam_im)
    ldt = rep(jnp.broadcast_to(log_dt[:, None], (N_GROUPS, P_STATE)))
    bre_t = jnp.transpose(b_re, (0, 2, 1)).reshape(rows, P_STATE)
    bim_t = jnp.transpose(b_im, (0, 2, 1)).reshape(rows, P_STATE)
    sds = jax.ShapeDtypeStruct((rows, P_STATE), F32)
    outs = pl.pallas_call(
        functools.partial(_disc_kernel, n_sq=n_sq),
        out_shape=[sds] * 6,
        name="s5_disc",
    )(lr, li, ldt, bre_t, bim_t)
    ar, ai, pr, pi, bbr_t, bbi_t = outs
    pick = lambda a: a.reshape(N_GROUPS, GROUP, P_STATE)[:, 0, :].reshape(N_BLK, 1, BLK_ST)
    abar = jnp.concatenate([pick(ar), pick(ai)], axis=1)
    apow = jnp.concatenate([pick(pr), pick(pi)], axis=1)
    return abar, apow, bbr_t, bbi_t


def _block_diag_weights(bbr_t, bbi_t, c_re, c_im):
    st = jnp.arange(BLK_ST)
    spread = (jnp.arange(P_STATE)[:, None] == st[None, :] % P_STATE).astype(F32)
    same_group = (jnp.arange(BLK_IN)[:, None] // GROUP == st[None, :] // P_STATE).astype(F32)
    hi = lax.Precision.HIGHEST

    def b_side(a):
        a = a.reshape(N_BLK, BLK_IN, P_STATE)
        return jnp.einsum('bkp,pn->bkn', a, spread, precision=hi) * same_group

    def c_side(a):
        a = a.reshape(N_BLK, BLK_IN, P_STATE)
        return jnp.einsum('pn,bkp->bnk', spread, a, precision=hi) * same_group.T

    bblk = jnp.concatenate([b_side(bbr_t), b_side(bbi_t)], axis=2).astype(BF16)
    cblk = jnp.concatenate([c_side(c_re), c_side(-c_im)], axis=1).astype(BF16)
    return bblk, cblk


def _cmul_add(ar, ai, sr, si, xr, xi):
    return ar * sr - ai * si + xr, ar * si + ai * sr + xi


def _ssm_prompt_kernel(u_ref, sz_ref, bblk_ref, cblk_ref, abar_ref, apow_ref, d_ref,
                       ys_ref, sre_ref, sim_ref,
                       x_scr, s_scr, y_scr, ynat_scr, carry_scr, perm_scr, *, tq):
    i = pl.program_id(1)
    seg_len = tq // SEGS
    n_half = BLK_ST // SCAN_W

    @pl.when(i == 0)
    def _():
        carry_scr[...] = jnp.zeros_like(carry_scr)

    @pl.when((pl.program_id(0) == 0) & (i == 0))
    def _():
        r = lax.broadcasted_iota(jnp.int32, (tq, tq), 0)
        c = lax.broadcasted_iota(jnp.int32, (tq, tq), 1)
        perm_scr[...] = jnp.where(c == (r % SEGS) * seg_len + r // SEGS, 1.0, 0.0).astype(BF16)

    u = u_ref[...]
    u_perm = jnp.dot(perm_scr[...], u, preferred_element_type=F32).astype(BF16)
    row_id = lax.broadcasted_iota(jnp.int32, (SEGS, SCAN_W), 0)

    for blk in range(N_BLK):
        par = blk % SSM_BUFS
        x_scr[par] = jnp.dot(u_perm[:, blk * BLK_IN:(blk + 1) * BLK_IN], bblk_ref[blk],
                             preferred_element_type=F32)
        for half in range(n_half):
            lo = half * SCAN_W
            re_cols = slice(lo, lo + SCAN_W)
            im_cols = slice(BLK_ST + lo, BLK_ST + lo + SCAN_W)
            ar = jnp.broadcast_to(abar_ref[blk, 0:1, lo:lo + SCAN_W], (SEGS, SCAN_W))
            ai = jnp.broadcast_to(abar_ref[blk, 1:2, lo:lo + SCAN_W], (SEGS, SCAN_W))

            def local_end(j, carry, par=par, re_cols=re_cols, im_cols=im_cols, ar=ar, ai=ai):
                row = pl.multiple_of(j * SEGS, SEGS)
                xr = x_scr[par, pl.ds(row, SEGS), re_cols]
                xi = x_scr[par, pl.ds(row, SEGS), im_cols]
                return _cmul_add(ar, ai, carry[0], carry[1], xr, xi)

            zero = jnp.zeros((SEGS, SCAN_W), F32)
            er, ei = lax.fori_loop(0, seg_len, local_end, (zero, zero), unroll=True)

            pr = apow_ref[blk, 0:1, lo:lo + SCAN_W]
            pi = apow_ref[blk, 1:2, lo:lo + SCAN_W]
            prev_r = carry_scr[blk, 0:1, lo:lo + SCAN_W]
            prev_i = carry_scr[blk, 1:2, lo:lo + SCAN_W]
            init_r = zero
            init_i = zero
            for k in range(SEGS):
                init_r = jnp.where(row_id == k, prev_r, init_r)
                init_i = jnp.where(row_id == k, prev_i, init_i)
                prev_r, prev_i = _cmul_add(pr, pi, prev_r, prev_i,
                                           er[k:k + 1, :], ei[k:k + 1, :])
            carry_scr[blk, 0:1, lo:lo + SCAN_W] = prev_r
            carry_scr[blk, 1:2, lo:lo + SCAN_W] = prev_i

            def full_scan(j, carry, par=par, re_cols=re_cols, im_cols=im_cols, ar=ar, ai=ai):
                row = pl.multiple_of(j * SEGS, SEGS)
                xr = x_scr[par, pl.ds(row, SEGS), re_cols]
                xi = x_scr[par, pl.ds(row, SEGS), im_cols]
                sr, si = _cmul_add(ar, ai, carry[0], carry[1], xr, xi)
                s_scr[par, pl.ds(row, SEGS), re_cols] = sr
                s_scr[par, pl.ds(row, SEGS), im_cols] = si
                return sr, si

            lax.fori_loop(0, seg_len, full_scan, (init_r, init_i), unroll=True)

        y = jnp.dot(s_scr[par].astype(BF16), cblk_ref[blk], preferred_element_type=F32)
        for sl in range(BLK_IN // 128):
            y_scr[blk * (BLK_IN // 128) + sl] = y[:, sl * 128:(sl + 1) * 128]

    for slab in range(E_SSM // 128):
        for k in range(SEGS):
            ynat_scr[k * seg_len:(k + 1) * seg_len, slab * 128:(slab + 1) * 128] = (
                y_scr[slab, pl.ds(k, seg_len, stride=SEGS), :])

    yy = (ynat_scr[...] + d_ref[...] * u.astype(F32)) * sz_ref[...].astype(F32)
    ys_ref[...] = jax.nn.gelu(yy).astype(BF16)
    for blk in range(N_BLK):
        sre_ref[0, :, blk * BLK_ST:(blk + 1) * BLK_ST] = carry_scr[blk, 0:1, :]
        sim_ref[0, :, blk * BLK_ST:(blk + 1) * BLK_ST] = carry_scr[blk, 1:2, :]


def _ssm_prompt(act, bblk, cblk, abar, apow, d_skip, *, tq):
    tiles = SEQ // tq
    st = jax.ShapeDtypeStruct((BATCH, 1, N_GROUPS * P_STATE), F32)
    const3 = lambda b, i: (0, 0, 0)
    return pl.pallas_call(
        functools.partial(_ssm_prompt_kernel, tq=tq),
        grid=(BATCH, tiles),
        in_specs=[
            pl.BlockSpec((tq, E_SSM), lambda b, i: (b * tiles + i, 1)),
            pl.BlockSpec((tq, E_SSM), lambda b, i: (b * tiles + i, 2)),
            pl.BlockSpec((N_BLK, BLK_IN, 2 * BLK_ST), const3),
            pl.BlockSpec((N_BLK, 2 * BLK_ST, BLK_IN), const3),
            pl.BlockSpec((N_BLK, 2, BLK_ST), const3),
            pl.BlockSpec((N_BLK, 2, BLK_ST), const3),
            pl.BlockSpec((1, E_SSM), lambda b, i: (0, 0)),
        ],
        out_specs=[
            pl.BlockSpec((tq, E_SSM), lambda b, i: (b * tiles + i, 0)),
            pl.BlockSpec((1, 1, N_GROUPS * P_STATE), lambda b, i: (b, 0, 0)),
            pl.BlockSpec((1, 1, N_GROUPS * P_STATE), lambda b, i: (b, 0, 0)),
        ],
        out_shape=[jax.ShapeDtypeStruct((BATCH * SEQ, E_SSM), BF16), st, st],
        scratch_shapes=[
            pltpu.VMEM((SSM_BUFS, tq, 2 * BLK_ST), F32),
            pltpu.VMEM((SSM_BUFS, tq, 2 * BLK_ST), F32),
            pltpu.VMEM((E_SSM // 128, tq, 128), F32),
            pltpu.VMEM((tq, E_SSM), F32),
            pltpu.VMEM((N_BLK, 2, BLK_ST), F32),
            pltpu.VMEM((tq, tq), BF16),
        ],
        compiler_params=_cparams(("arbitrary", "arbitrary")),
        name="ssm_prompt",
    )(act, act, bblk, cblk, abar, apow, d_skip)


def _ssm_sample_kernel(u_ref, sz_ref, bblk_ref, cblk_ref, abar_ref, d_ref, s0r_ref, s0i_ref,
                       ys_ref, sre_ref, sim_ref, x_scr, s_scr):
    n_half = BLK_ST // SCAN_W
    rows = DEC_SEQ * DEC_BATCH
    u = u_ref[...]
    ys = []
    for blk in range(N_BLK):
        x = jnp.dot(u[:, blk * BLK_IN:(blk + 1) * BLK_IN], bblk_ref[blk],
                    preferred_element_type=F32)
        for cch in range(2 * n_half):
            x_scr[cch] = x[:, cch * SCAN_W:(cch + 1) * SCAN_W]
        for half in range(n_half):
            lo = half * SCAN_W
            col = blk * BLK_ST + lo
            ar = jnp.broadcast_to(abar_ref[blk, 0:1, lo:lo + SCAN_W], (8, SCAN_W))
            ai = jnp.broadcast_to(abar_ref[blk, 1:2, lo:lo + SCAN_W], (8, SCAN_W))

            def body(rc, carry, half=half, col=col, ar=ar, ai=ai):
                r0 = pl.multiple_of(rc * 8, 8)
                sr = s0r_ref[pl.ds(r0, 8), col:col + SCAN_W]
                si = s0i_ref[pl.ds(r0, 8), col:col + SCAN_W]
                for t in range(DEC_SEQ):
                    row = pl.multiple_of(t * DEC_BATCH + r0, 8)
                    xr = x_scr[half, pl.ds(row, 8), :]
                    xi = x_scr[n_half + half, pl.ds(row, 8), :]
                    sr, si = _cmul_add(ar, ai, sr, si, xr, xi)
                    s_scr[half, pl.ds(row, 8), :] = sr
                    s_scr[n_half + half, pl.ds(row, 8), :] = si
                sre_ref[pl.ds(r0, 8), col:col + SCAN_W] = sr
                sim_ref[pl.ds(r0, 8), col:col + SCAN_W] = si
                return carry

            lax.fori_loop(0, DEC_BATCH // 8, body, 0)

        y = jnp.zeros((rows, BLK_IN), F32)
        for cch in range(2 * n_half):
            y = y + jnp.dot(s_scr[cch].astype(BF16),
                            cblk_ref[blk, cch * SCAN_W:(cch + 1) * SCAN_W, :],
                            preferred_element_type=F32)
        ys.append(y)
    y_all = jnp.concatenate(ys, axis=1)
    yy = (y_all + d_ref[...] * u.astype(F32)) * sz_ref[...].astype(F32)
    ys_ref[...] = jax.nn.gelu(yy).astype(BF16)


def _ssm_sample(act, bblk, cblk, abar, d_skip, s0r, s0i):
    rows = DEC_SEQ * DEC_BATCH
    n_half = BLK_ST // SCAN_W
    st = jax.ShapeDtypeStruct((DEC_BATCH, N_GROUPS * P_STATE), F32)
    full = lambda shape: pl.BlockSpec(shape, lambda i: tuple(0 for _ in shape))
    return pl.pallas_call(
        _ssm_sample_kernel,
        grid=(1,),
        in_specs=[
            pl.BlockSpec((rows, E_SSM), lambda i: (0, 1)),
            pl.BlockSpec((rows, E_SSM), lambda i: (0, 2)),
            full((N_BLK, BLK_IN, 2 * BLK_ST)),
            full((N_BLK, 2 * BLK_ST, BLK_IN)),
            full((N_BLK, 2, BLK_ST)),
            full((1, E_SSM)),
            full((DEC_BATCH, N_GROUPS * P_STATE)),
            full((DEC_BATCH, N_GROUPS * P_STATE)),
        ],
        out_specs=[full((rows, E_SSM)), full((DEC_BATCH, N_GROUPS * P_STATE)),
                   full((DEC_BATCH, N_GROUPS * P_STATE))],
        out_shape=[jax.ShapeDtypeStruct((rows, E_SSM), BF16), st, st],
        scratch_shapes=[
            pltpu.VMEM((2 * n_half, rows, SCAN_W), F32),
            pltpu.VMEM((2 * n_half, rows, SCAN_W), F32),
        ],
        compiler_params=_cparams(("arbitrary",)),
        name="ssm_sample",
    )(act, act, bblk, cblk, abar, d_skip, s0r, s0i)


def _softmax_rows(s):
    m = jnp.max(s, axis=-1, keepdims=True)
    e = jnp.exp(s - m)
    return e / jnp.sum(e, axis=-1, keepdims=True)


def _attn_prompt_kernel(q_ref, az_ref, k_ref, v_ref, o_ref):
    for h in range(N_HEADS):
        cols = slice(h * HEAD_DIM, (h + 1) * HEAD_DIM)
        kh = k_ref[:, cols].astype(BF16)
        vh = v_ref[:, cols].astype(BF16)
        s = lax.dot_general(q_ref[:, cols], kh, (((1,), (1,)), ((), ())),
                            preferred_element_type=F32)
        p = _softmax_rows(s).astype(BF16)
        o = jnp.dot(p, vh, preferred_element_type=F32)
        o_ref[:, cols] = (o * az_ref[:, cols].astype(F32)).astype(BF16)


def _attn_prompt(act, mk2d, mv2d, *, tr):
    tiles = SEQ // tr
    return pl.pallas_call(
        _attn_prompt_kernel,
        grid=(BATCH, tiles),
        in_specs=[
            pl.BlockSpec((tr, E_ATTN), lambda b, i: (b * tiles + i, 3)),
            pl.BlockSpec((tr, E_ATTN), lambda b, i: (b * tiles + i, 4)),
            pl.BlockSpec((MEM_LEN, E_ATTN), lambda b, i: (b, 0)),
            pl.BlockSpec((MEM_LEN, E_ATTN), lambda b, i: (b, 0)),
        ],
        out_specs=pl.BlockSpec((tr, E_ATTN), lambda b, i: (b * tiles + i, 0)),
        out_shape=jax.ShapeDtypeStruct((BATCH * SEQ, E_ATTN), BF16),
        compiler_params=_cparams(("arbitrary", "arbitrary")),
        name="attn_prompt",
    )(act, act, mk2d, mv2d)


def _attn_sample_kernel(q_ref, az_ref, k_ref, v_ref, o_ref, *, bt):
    n_kv = bt * MEM_LEN * N_HEADS
    k = k_ref[...].reshape(n_kv, HEAD_DIM).astype(BF16)
    v = v_ref[...].reshape(n_kv, HEAD_DIM).astype(BF16)
    rows = DEC_SEQ * bt
    q = jnp.concatenate(
        [q_ref[:, :, h * HEAD_DIM:(h + 1) * HEAD_DIM].reshape(rows, HEAD_DIM)
         for h in range(N_HEADS)], axis=0)
    s = lax.dot_general(q, k, (((1,), (1,)), ((), ())), preferred_element_type=F32)
    row = lax.broadcasted_iota(jnp.int32, s.shape, 0)
    col = lax.broadcasted_iota(jnp.int32, s.shape, 1)
    same = ((row // rows == col % N_HEADS)
            & (row % bt == col // (MEM_LEN * N_HEADS)))
    s = jnp.where(same, s, -1e30)
    p = _softmax_rows(s).astype(BF16)
    o = jnp.dot(p, v, preferred_element_type=F32)
    for h in range(N_HEADS):
        cols = slice(h * HEAD_DIM, (h + 1) * HEAD_DIM)
        az = az_ref[:, :, cols].reshape(rows, HEAD_DIM).astype(F32)
        o_ref[:, :, cols] = (o[h * rows:(h + 1) * rows] * az).astype(BF16).reshape(
            DEC_SEQ, bt, HEAD_DIM)


def _attn_sample(act3, k4, v4, *, bt):
    kv_spec = pl.BlockSpec((bt, MEM_LEN, N_HEADS, HEAD_DIM), lambda b: (b, 0, 0, 0))
    return pl.pallas_call(
        functools.partial(_attn_sample_kernel, bt=bt),
        grid=(DEC_BATCH // bt,),
        in_specs=[
            pl.BlockSpec((DEC_SEQ, bt, E_ATTN), lambda b: (0, b, 3)),
            pl.BlockSpec((DEC_SEQ, bt, E_ATTN), lambda b: (0, b, 4)),
            kv_spec, kv_spec,
        ],
        out_specs=pl.BlockSpec((DEC_SEQ, bt, E_ATTN), lambda b: (0, b, 0)),
        out_shape=jax.ShapeDtypeStruct((DEC_SEQ, DEC_BATCH, E_ATTN), BF16),
        compiler_params=_cparams(("arbitrary",)),
        name="attn_sample",
    )(act3, act3, k4, v4)


def _merge_kernel(gated_ref, ys_ref, oz_ref, g0_ref, g1_ref, g2_ref,
                  wco_ref, wga_ref, wgb_ref, wao_ref, o_ref, *wbf_refs, emit_w):
    w_refs = (wco_ref, wga_ref, wgb_ref, wao_ref)
    if emit_w:
        @pl.when(pl.program_id(1) == 0)
        def _():
            for w_ref, wbf_ref in zip(w_refs, wbf_refs):
                wbf_ref[...] = w_ref[...].astype(BF16)

        w_refs = wbf_refs

    dot = functools.partial(jnp.dot, preferred_element_type=F32)
    conv_out = dot(gated_ref[...], w_refs[0][...])
    ys = ys_ref[...]
    ssm_out = dot(ys, w_refs[1][...]) * _sigmoid(dot(ys, w_refs[2][...]))
    attn_out = dot(oz_ref[...], w_refs[3][...])
    merged = (g0_ref[...].astype(F32) * conv_out + g1_ref[...].astype(F32) * ssm_out
              + g2_ref[...].astype(F32) * attn_out)
    o_ref[...] = merged.astype(BF16)


def _merge(act, ys, oz, wco, wga, wgb, wao, *, tm):
    rows = act.shape[0]
    tn = 1024
    nn = D_MODEL // tn
    emit_w = wco.dtype == F32
    wspec = pl.BlockSpec((E_CONV, tn), lambda n, i: (0, n),
                         pipeline_mode=pl.Buffered(1) if emit_w else None)
    o_spec = pl.BlockSpec((tm, tn), lambda n, i: (i, n))
    o_shape = jax.ShapeDtypeStruct((rows, D_MODEL), BF16)
    wbf_specs = [pl.BlockSpec((E_CONV, tn), lambda n, i: (0, n))] * 4 if emit_w else []
    wbf_shapes = [jax.ShapeDtypeStruct((E_CONV, D_MODEL), BF16)] * 4 if emit_w else []
    return pl.pallas_call(
        functools.partial(_merge_kernel, emit_w=emit_w),
        grid=(nn, rows // tm),
        in_specs=[
            pl.BlockSpec((tm, CB), lambda n, i: (i, 0)),
            pl.BlockSpec((tm, E_SSM), lambda n, i: (i, 0)),
            pl.BlockSpec((tm, E_ATTN), lambda n, i: (i, 0)),
            pl.BlockSpec((tm, tn), lambda n, i: (i, 5 + n)),
            pl.BlockSpec((tm, tn), lambda n, i: (i, 5 + nn + n)),
            pl.BlockSpec((tm, tn), lambda n, i: (i, 5 + 2 * nn + n)),
            wspec, wspec, wspec, wspec,
        ],
        out_specs=[o_spec] + wbf_specs,
        out_shape=[o_shape] + wbf_shapes,
        compiler_params=_cparams(("arbitrary", "arbitrary")),
        name="merge",
    )(act, ys, oz, act, act, act, wco, wga, wgb, wao)


def _out_kernel(x_ref, m_ref, w_ref, g_ref, o_ref, *rest, emit_w):
    if emit_w:
        (wbf_ref,) = rest

        @pl.when(pl.program_id(0) == 0)
        def _():
            wbf_ref[...] = w_ref[...].astype(BF16)

        w_ref = wbf_ref
    y = x_ref[...] + jnp.dot(m_ref[...], w_ref[...], preferred_element_type=F32)
    o_ref[...] = _rms(y, g_ref[...])


def _outproj(x2d, merged, w, g, *, tm):
    rows = x2d.shape[0]
    emit_w = w.dtype == F32
    w_spec = pl.BlockSpec((D_MODEL, D_MODEL), lambda i: (0, 0),
                          pipeline_mode=pl.Buffered(1) if emit_w else None)
    y_spec = pl.BlockSpec((tm, D_MODEL), lambda i: (i, 0))
    y_shape = jax.ShapeDtypeStruct((rows, D_MODEL), F32)
    return pl.pallas_call(
        functools.partial(_out_kernel, emit_w=emit_w),
        grid=(rows // tm,),
        in_specs=[
            pl.BlockSpec((tm, D_MODEL), lambda i: (i, 0)),
            pl.BlockSpec((tm, D_MODEL), lambda i: (i, 0)),
            w_spec,
            pl.BlockSpec((1, D_MODEL), lambda i: (0, 0)),
        ],
        out_specs=[y_spec, pl.BlockSpec((D_MODEL, D_MODEL), lambda i: (0, 0))] if emit_w else y_spec,
        out_shape=[y_shape, jax.ShapeDtypeStruct((D_MODEL, D_MODEL), BF16)] if emit_w else y_shape,
        compiler_params=_cparams(("arbitrary",)),
        name="outproj",
    )(x2d, merged, w, g)


def kernel(x_prompt, x_sample, mem_prompt, cache_mem_k, cache_mem_v, state_conv, state_ssm_re, state_ssm_im, norm_g, mem_norm_g, w_in, conv_w, w_conv_out, ssm_lambda_re, ssm_lambda_im, ssm_log_dt, ssm_b_re, ssm_b_im, ssm_c_re, ssm_c_im, ssm_d, w_glu_a, w_glu_b, w_mem_k, w_mem_v, w_attn_out, w_out, final_norm_g):
    l = 0
    tq = TQ_SSM
    w_branch = (w_conv_out[l], w_glu_a[l], w_glu_b[l], w_attn_out[l])
    g_in = norm_g[l][None, :]
    g_fin = final_norm_g[None, :]
    d_skip = ssm_d[l][None, :]

    n_sq = int(math.log2(tq // SEGS))
    abar, apow, bbr_t, bbi_t = _discretise(ssm_lambda_re[l], ssm_lambda_im[l], ssm_log_dt[l],
                                           ssm_b_re[l], ssm_b_im[l], n_sq)
    bblk, cblk = _block_diag_weights(bbr_t, bbi_t, ssm_c_re[l], ssm_c_im[l])

    rows_s = DEC_SEQ * DEC_BATCH
    xs = jnp.transpose(x_sample, (1, 0, 2)).reshape(rows_s, D_MODEL)
    halo_s = jnp.transpose(state_conv[l], (1, 0, 2)).reshape(1, 2 * DEC_BATCH, E_CONV)
    act_s, nconv_s = _inproj(xs, g_in, w_in[l], halo_s, conv_w[l], tm=rows_s,
                             shift=DEC_BATCH, pad=2 * DEC_BATCH, tiles_per_seq=1)

    xp = x_prompt.reshape(BATCH * SEQ, D_MODEL)
    tm_p = TM_INPROJ
    act_p, nconv_p = _inproj(xp, g_in, w_in[l], jnp.zeros((BATCH, 2, E_CONV), F32), conv_w[l],
                             tm=tm_p, shift=1, pad=8, tiles_per_seq=SEQ // tm_p)
    mem2d = mem_prompt.reshape(BATCH * MEM_LEN, D_MODEL)
    g_mem = mem_norm_g[l][None, :]
    mk, mv = _memproj(mem2d, g_mem, w_mem_k[l], w_mem_v[l])
    ys_p, sre_p, sim_p = _ssm_prompt(act_p, bblk, cblk, abar, apow, d_skip, tq=tq)
    oz_p = _attn_prompt(act_p, mk, mv, tr=TR_ATTN)
    merged_p, *w_branch_bf = _merge(act_p, ys_p, oz_p, *w_branch, tm=TM_TAIL)
    y_p, wo_bf = _outproj(xp, merged_p, w_out[l], g_fin, tm=TM_TAIL)

    s0r = state_ssm_re[l].reshape(DEC_BATCH, N_GROUPS * P_STATE)
    s0i = state_ssm_im[l].reshape(DEC_BATCH, N_GROUPS * P_STATE)
    ys_s, sre_s, sim_s = _ssm_sample(act_s, bblk, cblk, abar, d_skip, s0r, s0i)
    oz_s = _attn_sample(act_s.reshape(DEC_SEQ, DEC_BATCH, N_ACT_BLOCKS * CB),
                        cache_mem_k[l], cache_mem_v[l], bt=BT_ATTN)
    (merged_s,) = _merge(act_s, ys_s, oz_s.reshape(rows_s, E_ATTN), *w_branch_bf, tm=TM_TAIL)
    y_s = _outproj(xs, merged_s, wo_bf, g_fin, tm=TM_TAIL)

    y_prompt = y_p.reshape(BATCH, SEQ, D_MODEL)
    y_sample = jnp.transpose(y_s.reshape(DEC_SEQ, DEC_BATCH, D_MODEL), (1, 0, 2))
    st_shape_p = (1, BATCH, N_GROUPS, P_STATE)
    st_shape_s = (1, DEC_BATCH, N_GROUPS, P_STATE)
    new_conv_s = jnp.transpose(nconv_s.reshape(2, DEC_BATCH, E_CONV), (1, 0, 2))
    return (y_prompt, y_sample,
            mk.reshape(1, BATCH, MEM_LEN, N_HEADS, HEAD_DIM),
            mv.reshape(1, BATCH, MEM_LEN, N_HEADS, HEAD_DIM),
            nconv_p[None],
            sre_p.reshape(st_shape_p), sim_p.reshape(st_shape_p),
            new_conv_s[None],
            sre_s.reshape(st_shape_s), sim_s.reshape(st_shape_s))
```
